```python
import math
import jax, jax.numpy as jnp
from jax import lax
import numpy as np

D_MODEL = 2048
BATCH = 1
SEQ = 8192
DEPTH = 1
DEC_BATCH = 32
DEC_SEQ = 4
PAST_LEN = 16384
PAGE_SIZE = 128

A_HEADS = 16
A_HEAD_DIM = 64
A_WIDTH = A_HEADS * A_HEAD_DIM
DECAY_LORA = 64
ICLR_LORA = 64
GATE_LORA = 128
A_COLS = 3 * A_WIDTH + DECAY_LORA + ICLR_LORA + GATE_LORA
GN_EPS = 64e-5
B_HEADS = 8
B_QK_DIM = 64
B_V_DIM = 2 * B_QK_DIM
B_QK_WIDTH = B_HEADS * 2 * B_QK_DIM
B_V_WIDTH = B_HEADS * B_V_DIM
B_COLS = 2 * B_QK_WIDTH + B_V_WIDTH
Q_BLOCK = 128
NUM_BUCKETS = 32
MAX_DISTANCE = 128
IN_COLS = A_COLS + B_COLS + 2 * D_MODEL
IN_SPLITS = [A_COLS, A_COLS + B_QK_WIDTH, A_COLS + 2 * B_QK_WIDTH, A_COLS + B_COLS, A_COLS + B_COLS + D_MODEL]
A_SPLITS = [A_WIDTH, 2 * A_WIDTH, 3 * A_WIDTH, 3 * A_WIDTH + DECAY_LORA, 3 * A_WIDTH + DECAY_LORA + ICLR_LORA]
N_GROUPS = 4
EXPERTS_PER_GROUP = 8
N_EXPERTS = N_GROUPS * EXPERTS_PER_GROUP
TOP_K = 2
EXPERT_HIDDEN = 256
NORM_EPS = 1e-6
NEG_INF = -1e30

kernel_name = 'hybrid_rwkv7_diffattn_hmoe_step'


def _rms(x, eps=NORM_EPS):
    x32 = x.astype(jnp.float32)
    return x32 * lax.rsqrt(jnp.mean(x32 * x32, axis=-1, keepdims=True) + eps)


def _modulated_norm(x, gain, shift, scale):
    h = _rms(x) * gain.astype(jnp.float32)
    h = h * (1.0 + scale[:, None, :].astype(jnp.float32)) + shift[:, None, :].astype(jnp.float32)
    return h.astype(x.dtype)


def _rel_bucket(dist):
    n = jnp.maximum(dist, 0)
    max_exact = NUM_BUCKETS // 2
    nf = jnp.maximum(n, 1).astype(jnp.float32)
    large = max_exact + (jnp.log(nf / max_exact) / math.log(MAX_DISTANCE / max_exact)
                         * (NUM_BUCKETS - max_exact)).astype(jnp.int32)
    large = jnp.minimum(large, NUM_BUCKETS - 1)
    return jnp.where(n < max_exact, n, large)


def _diff_attn_core(q, k, v, q_pos, k_pos, rel_bias, lam):
    s = jnp.einsum('qhmd,khmd->hmqk', q.astype(jnp.float32), k.astype(jnp.float32)) * (B_QK_DIM ** -0.5)
    dist = q_pos[:, None] - k_pos[None, :]
    bias = jnp.transpose(rel_bias.astype(jnp.float32)[_rel_bucket(dist)], (2, 0, 1))
    s = jnp.where((dist >= 0)[None, None], s + bias[:, None], NEG_INF)
    p = jax.nn.softmax(s, axis=-1)
    attn = p[:, 0] - lam * p[:, 1]
    return jnp.einsum('hqk,khd->qhd', attn, v.astype(jnp.float32))


def _attend_prompt(q, k, v, rel_bias, lam):
    b, t = q.shape[0], q.shape[1]
    nb = t // Q_BLOCK
    qb = jnp.moveaxis(q.reshape(b, nb, Q_BLOCK, B_HEADS, 2, B_QK_DIM), 1, 0)
    k_pos = jnp.arange(t, dtype=jnp.int32)

    def one_block(args):
        q_blk, i = args
        q_pos = i * Q_BLOCK + jnp.arange(Q_BLOCK, dtype=jnp.int32)
        return jax.vmap(lambda qq, kk, vv: _diff_attn_core(qq, kk, vv, q_pos, k_pos, rel_bias, lam))(q_blk, k, v)

    o = lax.map(one_block, (qb, jnp.arange(nb, dtype=jnp.int32)))
    return jnp.moveaxis(o, 0, 1).reshape(b, t, B_HEADS, B_V_DIM)


def _attend_sample(q, k, v, cache_k, cache_v, layer, page_table, rel_bias, lam):
    n_past = page_table.shape[1] * PAGE_SIZE
    s_new = q.shape[1]
    q_pos = n_past + jnp.arange(s_new, dtype=jnp.int32)
    k_pos = jnp.arange(n_past + s_new, dtype=jnp.int32)

    def one_seq(args):
        qq, kk, vv, pages = args
        k_past = cache_k[layer, pages].reshape(n_past, B_HEADS, 2, B_QK_DIM).astype(kk.dtype)
        v_past = cache_v[layer, pages].reshape(n_past, B_HEADS, B_V_DIM).astype(vv.dtype)
        k_all = jnp.concatenate([k_past, kk], axis=0)
        v_all = jnp.concatenate([v_past, vv], axis=0)
        return _diff_attn_core(qq, k_all, v_all, q_pos, k_pos, rel_bias, lam)

    return lax.map(one_seq, (q, k, v, page_table))


def _rwkv_recurrence(r, w, k, v, kk, a, s0):
    def step(S, xs):
        r_t, w_t, k_t, v_t, kk_t, a_t = xs
        sa = jnp.einsum('bhij,bhj->bhi', S, -kk_t)
        S = (S * w_t[:, :, None, :] + sa[..., :, None] * (kk_t * a_t)[..., None, :]
             + v_t[..., :, None] * k_t[..., None, :])
        return S, jnp.einsum('bhij,bhj->bhi', S, r_t)

    xs = tuple(jnp.moveaxis(z, 1, 0) for z in (r, w, k, v, kk, a))
    s_fin, y = lax.scan(step, s0, xs)
    return jnp.moveaxis(y, 0, 1), s_fin


def _moe(h, p):
    b, t, d = h.shape
    tok = h.reshape(b * t, d)
    f32 = jnp.float32
    lc = (tok @ p['w_rc']).astype(f32) + p['b_rc'].astype(f32)
    pc = jax.nn.softmax(lc, axis=-1)
    _, g_idx = lax.top_k(lc, 1)
    group_oh = jax.nn.one_hot(g_idx[:, 0], N_GROUPS, dtype=f32)
    p_g = jnp.sum(pc * group_oh, axis=-1, keepdims=True)
    lf = jnp.einsum('nd,gde->nge', tok, p['w_rf']).astype(f32) + p['b_rf'].astype(f32)
    lf_sel = jnp.sum(lf * group_oh[:, :, None], axis=1)
    top_v, top_i = lax.top_k(lf_sel, TOP_K)
    p_in = jax.nn.softmax(top_v, axis=-1)
    w_grp = jnp.sum(jax.nn.one_hot(top_i, EXPERTS_PER_GROUP, dtype=f32) * p_in[..., None], axis=1)
    comb = (group_oh[:, :, None] * (p_g * w_grp)[:, None, :]).reshape(b * t, N_EXPERTS)
    hg = jnp.einsum('nd,edf->nef', tok, p['w_eg'])
    hu = jnp.einsum('nd,edf->nef', tok, p['w_eu'])
    act = jax.nn.silu(hg) * hu * comb[:, :, None].astype(hg.dtype)
    out = jnp.einsum('nef,efd->nd', act, p['w_ed'])
    return out.reshape(b, t, d)


def _layer(x, c, p, lam_init, wkv0, shift0, attend):
    f32 = jnp.float32
    b, t, _ = x.shape
    ada = jax.nn.silu(c) @ p['w_ada'] + p['b_ada']
    sh1, sc1, g1, sh2, sc2, g2 = jnp.split(ada, 6, axis=-1)
    h = _modulated_norm(x, p['norm1'], sh1, sc1)
    u = h @ p['w_in']
    ua, uq, uk, uv, ga, gb = jnp.split(u, IN_SPLITS, axis=-1)

    ua32 = ua.astype(f32)
    prev = jnp.concatenate([shift0[:, None].astype(f32), ua32[:, :-1]], axis=1)
    us = ua32 + (prev - ua32) * p['mu_shift']
    r, kr, vr, wd, ad, gd = jnp.split(us, A_SPLITS, axis=-1)
    w_log = -jax.nn.softplus(-(p['w0'] + jnp.tanh(wd) @ p['w_decay_up'])) - 0.5
    decay = jnp.exp(-jnp.exp(w_log))
    iclr = jax.nn.sigmoid(p['a0'] + ad @ p['w_iclr_up'])
    gate_o = jax.nn.sigmoid(gd) @ p['w_gate_up']
    heads = lambda z: z.reshape(b, t, A_HEADS, A_HEAD_DIM)
    kk = heads(kr * p['k_k'])
    kk = kk / jnp.maximum(jnp.sqrt(jnp.sum(kk * kk, axis=-1, keepdims=True)), 1e-12)
    k_mod = kr * (1.0 + (iclr - 1.0) * p['k_a'])
    rh, kh, vh, ah, wh = heads(r), heads(k_mod), heads(vr), heads(iclr), heads(decay)
    y, wkv_new = _rwkv_recurrence(rh, wh, kh, vh, kk, ah, wkv0.astype(f32))
    mu = jnp.mean(y, axis=-1, keepdims=True)
    var = jnp.mean(jnp.square(y - mu), axis=-1, keepdims=True)
    y = ((y - mu) * lax.rsqrt(var + GN_EPS)).reshape(b, t, A_WIDTH) * p['ln_x_w'] + p['ln_x_b']
    bonus = jnp.sum(rh * kh * p['r_k'], axis=-1, keepdims=True) * vh
    y_a = ((y + bonus.reshape(b, t, A_WIDTH)) * gate_o).astype(x.dtype)
    shift_new = ua[:, -1]

    q = _rms(uq.reshape(b, t, B_HEADS, 2, B_QK_DIM)) * p['q_norm'].astype(f32)
    k = _rms(uk.reshape(b, t, B_HEADS, 2, B_QK_DIM)) * p['k_norm'].astype(f32)
    k = k.astype(x.dtype)
    v = uv.reshape(b, t, B_HEADS, B_V_DIM)
    lam = (jnp.exp(jnp.sum(p['lambda_q1'].astype(f32) * p['lambda_k1'].astype(f32)))
           - jnp.exp(jnp.sum(p['lambda_q2'].astype(f32) * p['lambda_k2'].astype(f32))) + lam_init)
    o = attend(q.astype(x.dtype), k, v, lam)
    o = _rms(o) * p['subln'].astype(f32) * (1.0 - lam_init)
    y_b = o.reshape(b, t, B_V_WIDTH).astype(x.dtype)

    merged = jax.nn.sigmoid(ga) * (y_a @ p['w_branch_a']) + jax.nn.sigmoid(gb) * (y_b @ p['w_branch_b'])
    x = x + g1[:, None] * (merged @ p['w_out'])
    h2 = _modulated_norm(x, p['norm2'], sh2, sc2)
    x = x + g2[:, None] * _moe(h2, p)
    k_rows = k.reshape(b, t, B_HEADS, 2 * B_QK_DIM)
    return x, k_rows, v, wkv_new.astype(wkv0.dtype), shift_new


def setup_inputs(seed: int = 0) -> dict:
    key = jax.random.key(seed)
    ks = iter(jax.random.split(key, 64))
    nrm = lambda shape, scale: jax.random.normal(next(ks), shape, jnp.float32) * scale
    L = DEPTH
    n_pages = PAST_LEN // PAGE_SIZE
    n_used = DEC_BATCH * n_pages
    n_phys = n_used + max(1, n_used // 4)
    page_table = jax.random.permutation(next(ks), n_phys)[:n_used].reshape(DEC_BATCH, n_pages).astype(jnp.int32)
    D = D_MODEL
    return {
        'x_prompt': nrm((BATCH, SEQ, D), 1.0),
        'x_sample': nrm((DEC_BATCH, DEC_SEQ, D), 1.0),
        'c_prompt': nrm((BATCH, D), 1.0),
        'c_sample': nrm((DEC_BATCH, D), 1.0),
        'cache_k': nrm((L, n_phys, PAGE_SIZE, B_HEADS, 2 * B_QK_DIM), 1.0),
        'cache_v': nrm((L, n_phys, PAGE_SIZE, B_HEADS, B_V_DIM), 1.0),
        'state_wkv': nrm((L, DEC_BATCH, A_HEADS, A_HEAD_DIM, A_HEAD_DIM), 1.0),
        'state_shift': nrm((L, DEC_BATCH, A_COLS), 1.0),
        'page_table': page_table,
        'rel_bias': nrm((NUM_BUCKETS, B_HEADS), 0.5),
        'w_ada': nrm((L, D, 6 * D), 0.5 * D ** -0.5),
        'b_ada': nrm((L, 6 * D), 0.05),
        'norm1': 1.0 + nrm((L, D), 0.05),
        'norm2': 1.0 + nrm((L, D), 0.05),
        'w_in': nrm((L, D, IN_COLS), D ** -0.5),
        'mu_shift': jax.random.uniform(next(ks), (L, A_COLS), jnp.float32),
        'w0': nrm((L, A_WIDTH), 0.5),
        'w_decay_up': nrm((L, DECAY_LORA, A_WIDTH), 0.5 * DECAY_LORA ** -0.5),
        'a0': nrm((L, A_WIDTH), 0.5),
        'w_iclr_up': nrm((L, ICLR_LORA, A_WIDTH), 0.5 * ICLR_LORA ** -0.5),
        'w_gate_up': nrm((L, GATE_LORA, A_WIDTH), GATE_LORA ** -0.5),
        'k_k': 0.85 + nrm((L, A_WIDTH), 0.05),
        'k_a': 1.0 + nrm((L, A_WIDTH), 0.05),
        'r_k': nrm((L, A_HEADS, A_HEAD_DIM), 0.1),
        'ln_x_w': 1.0 + nrm((L, A_WIDTH), 0.05),
        'ln_x_b': nrm((L, A_WIDTH), 0.02),
        'q_norm': 1.0 + nrm((L, 2, B_QK_DIM), 0.05),
        'k_norm': 1.0 + nrm((L, 2, B_QK_DIM), 0.05),
        'lambda_q1': nrm((L, B_QK_DIM), 0.1),
        'lambda_k1': nrm((L, B_QK_DIM), 0.1),
        'lambda_q2': nrm((L, B_QK_DIM), 0.1),
        'lambda_k2': nrm((L, B_QK_DIM), 0.1),
        'subln': 1.0 + nrm((L, B_V_DIM), 0.05),
        'w_branch_a': nrm((L, A_WIDTH, D), A_WIDTH ** -0.5),
        'w_branch_b': nrm((L, B_V_WIDTH, D), B_V_WIDTH ** -0.5),
        'w_out': nrm((L, D, D), D ** -0.5),
        'w_router_coarse': nrm((L, D, N_GROUPS), D ** -0.5),
        'b_router_coarse': nrm((L, N_GROUPS), 0.01),
        'w_router_fine': nrm((L, N_GROUPS, D, EXPERTS_PER_GROUP), D ** -0.5),
        'b_router_fine': nrm((L, N_GROUPS, EXPERTS_PER_GROUP), 0.01),
        'w_exp_gate': nrm((L, N_EXPERTS, D, EXPERT_HIDDEN), D ** -0.5),
        'w_exp_up': nrm((L, N_EXPERTS, D, EXPERT_HIDDEN), D ** -0.5),
        'w_exp_down': nrm((L, N_EXPERTS, EXPERT_HIDDEN, D), EXPERT_HIDDEN ** -0.5),
    }


def reference(x_prompt, x_sample, c_prompt, c_sample, cache_k, cache_v, state_wkv, state_shift, page_table,
              rel_bias, w_ada, b_ada, norm1, norm2, w_in, mu_shift, w0, w_decay_up, a0, w_iclr_up, w_gate_up,
              k_k, k_a, r_k, ln_x_w, ln_x_b, q_norm, k_norm, lambda_q1, lambda_k1, lambda_q2, lambda_k2, subln,
              w_branch_a, w_branch_b, w_out, w_router_coarse, b_router_coarse, w_router_fine, b_router_fine,
              w_exp_gate, w_exp_up, w_exp_down):
    yp, ys = x_prompt, x_sample
    kp_l, vp_l, sp_l, hp_l, ks_l, vs_l, ss_l, hs_l = [], [], [], [], [], [], [], []
    for l in range(DEPTH):
        p = dict(w_ada=w_ada[l], b_ada=b_ada[l], norm1=norm1[l], norm2=norm2[l], w_in=w_in[l],
                 mu_shift=mu_shift[l], w0=w0[l], w_decay_up=w_decay_up[l], a0=a0[l], w_iclr_up=w_iclr_up[l],
                 w_gate_up=w_gate_up[l], k_k=k_k[l], k_a=k_a[l], r_k=r_k[l], ln_x_w=ln_x_w[l], ln_x_b=ln_x_b[l],
                 q_norm=q_norm[l], k_norm=k_norm[l], lambda_q1=lambda_q1[l], lambda_k1=lambda_k1[l],
                 lambda_q2=lambda_q2[l], lambda_k2=lambda_k2[l], subln=subln[l],
                 w_branch_a=w_branch_a[l], w_branch_b=w_branch_b[l], w_out=w_out[l],
                 w_rc=w_router_coarse[l], b_rc=b_router_coarse[l], w_rf=w_router_fine[l], b_rf=b_router_fine[l],
                 w_eg=w_exp_gate[l], w_eu=w_exp_up[l], w_ed=w_exp_down[l])
        lam_init = 0.8 - 0.6 * math.exp(-0.3 * l)
        wkv0_p = jnp.zeros((yp.shape[0], A_HEADS, A_HEAD_DIM, A_HEAD_DIM), yp.dtype)
        shift0_p = jnp.zeros((yp.shape[0], A_COLS), yp.dtype)
        yp, kp, vp, sp, hp = _layer(yp, c_prompt, p, lam_init, wkv0_p, shift0_p,
                                    lambda q, k, v, lam: _attend_prompt(q, k, v, rel_bias, lam))
        ys, kс, vs, ss, hs = _layer(ys, c_sample, p, lam_init, state_wkv[l], state_shift[l],
                                    lambda q, k, v, lam, layer=l: _attend_sample(q, k, v, cache_k, cache_v, layer,
                                                                                 page_table, rel_bias, lam))
        kp_l.append(kp); vp_l.append(vp); sp_l.append(sp); hp_l.append(hp)
        ks_l.append(kс); vs_l.append(vs); ss_l.append(ss); hs_l.append(hs)
    k_rows_prompt = jnp.stack(kp_l)
    v_rows_prompt = jnp.stack(vp_l)
    wkv_prompt = jnp.stack(sp_l)
    shift_prompt = jnp.stack(hp_l)
    k_rows_sample = jnp.stack(ks_l)
    v_rows_sample = jnp.stack(vs_l)
    wkv_sample = jnp.stack(ss_l)
    shift_sample = jnp.stack(hs_l)
    return (yp, ys, k_rows_prompt, v_rows_prompt, wkv_prompt, shift_prompt,
            k_rows_sample, v_rows_sample, wkv_sample, shift_sample)
```

```python
import functools
import math

import jax
import jax.numpy as jnp
from jax import lax
from jax.experimental import pallas as pl
from jax.experimental.pallas import tpu as pltpu

F32 = jnp.float32
BF16 = jnp.bfloat16

A_HEADS = 16
A_HEAD_DIM = 64
A_WIDTH = A_HEADS * A_HEAD_DIM
DECAY_LORA = 64
ICLR_LORA = 64
GATE_LORA = 128
LORA_IN = DECAY_LORA + ICLR_LORA + GATE_LORA
A_COLS = 3 * A_WIDTH + LORA_IN
GN_EPS = 64e-5
B_HEADS = 8
B_QK_DIM = 64
B_V_DIM = 2 * B_QK_DIM
B_QK_WIDTH = B_HEADS * 2 * B_QK_DIM
B_V_WIDTH = B_HEADS * B_V_DIM
NUM_BUCKETS = 32
MAX_DISTANCE = 128
N_GROUPS = 4
EXPERTS_PER_GROUP = 8
N_EXPERTS = N_GROUPS * EXPERTS_PER_GROUP
NORM_EPS = 1e-6
NEG_INF = -1e30

LANES = 128
ROUTER_LANES = LANES
VMEM_LIMIT = 56 * 1024 * 1024


def _cparams(*sem):
    return pltpu.CompilerParams(dimension_semantics=sem, vmem_limit_bytes=VMEM_LIMIT)


def _dot(a, b):
    return jnp.dot(a, b, preferred_element_type=F32)


def _dot_nt(a, b):
    return lax.dot_general(a, b, (((1,), (1,)), ((), ())), preferred_element_type=F32)


def _dot_tn(a, b):
    return lax.dot_general(a, b, (((0,), (0,)), ((), ())), preferred_element_type=F32)


def _split3(x):
    hi = x.astype(BF16)
    r1 = x - hi.astype(F32)
    mid = r1.astype(BF16)
    lo = (r1 - mid.astype(F32)).astype(BF16)
    return hi, mid, lo


def _dot_sel_rhs(x, sel):
    hi, mid, lo = _split3(x)
    return _dot(hi, sel) + _dot(mid, sel) + _dot(lo, sel)


def _dot_sel_lhs(sel, x):
    hi, mid, lo = _split3(x)
    return _dot(sel, hi) + _dot(sel, mid) + _dot(sel, lo)


def _group_sum(x, sel):
    parts = [_dot_sel_rhs(x[:, t * LANES:(t + 1) * LANES], sel) for t in range(x.shape[1] // LANES)]
    return parts[0] if len(parts) == 1 else jnp.concatenate(parts, axis=1)


def _sigmoid(x):
    return 1.0 / (1.0 + jnp.exp(-x))


def _silu(x):
    return x * _sigmoid(x)


def _rel_bias_minus_far(dist, table_fn):
    n = jnp.maximum(dist, 0)
    max_exact = NUM_BUCKETS // 2
    nf = jnp.maximum(n, 1).astype(F32)
    large = max_exact + (jnp.log(nf / max_exact) / math.log(MAX_DISTANCE / max_exact)
                         * (NUM_BUCKETS - max_exact)).astype(jnp.int32)
    large = jnp.minimum(large, NUM_BUCKETS - 1)
    bucket = jnp.where(n < max_exact, n, large)
    far = table_fn(NUM_BUCKETS - 1)
    out = jnp.zeros(dist.shape, F32)
    for b in range(NUM_BUCKETS - 1):
        out = jnp.where(bucket == b, table_fn(b) - far, out)
    return out


def _ada_kernel(c_ref, w_ref, b_ref, o_ref):
    s = _silu(c_ref[...])
    o_ref[...] = _dot(s.astype(BF16), w_ref[...].astype(BF16)) + b_ref[...]


def _ada(c, w, b):
    m, d = c.shape
    n = w.shape[1]
    tn = 1024
    return pl.pallas_call(
        _ada_kernel,
        grid=(n // tn,),
        in_specs=[pl.BlockSpec((m, d), lambda j: (0, 0)),
                  pl.BlockSpec((d, tn), lambda j: (0, j)),
                  pl.BlockSpec((1, tn), lambda j: (0, j))],
        out_specs=pl.BlockSpec((m, tn), lambda j: (0, j)),
        out_shape=jax.ShapeDtypeStruct((m, n), F32),
        compiler_params=_cparams("arbitrary"),
        name="ada_modulation",
    )(c, w, b.reshape(1, n))


def _modnorm(x, gain, scale, shift):
    ms = jnp.mean(x * x, axis=-1, keepdims=True)
    h = x * lax.rsqrt(ms + NORM_EPS) * gain
    return h * (1.0 + scale) + shift


def _norm_proj_kernel(x_ref, gain_ref, scale_ref, shift_ref, w_ref, o_ref, h_scr):
    @pl.when(pl.program_id(1) == 0)
    def _():
        h_scr[...] = _modnorm(x_ref[...], gain_ref[...], scale_ref[...], shift_ref[...]).astype(BF16)

    o_ref[...] = _dot(h_scr[...], w_ref[...])


def _mod_spec(mod, tm):
    if mod.shape[0] == 1:
        return pl.BlockSpec((1, mod.shape[1]), lambda i, *_: (0, 0))
    return pl.BlockSpec((tm, mod.shape[1]), lambda i, *_: (i, 0))


def _norm_proj(x, gain, scale, shift, w, tm, tn):
    m, d = x.shape
    n = w.shape[1]
    return pl.pallas_call(
        _norm_proj_kernel,
        grid=(m // tm, n // tn),
        in_specs=[pl.BlockSpec((tm, d), lambda i, j: (i, 0)),
                  pl.BlockSpec((1, d), lambda i, j: (0, 0)),
                  _mod_spec(scale, tm), _mod_spec(shift, tm),
                  pl.BlockSpec((d, tn), lambda i, j: (0, j))],
        out_specs=pl.BlockSpec((tm, tn), lambda i, j: (i, j)),
        out_shape=jax.ShapeDtypeStruct((m, n), F32),
        scratch_shapes=[pltpu.VMEM((tm, d), BF16)],
        compiler_params=_cparams("arbitrary", "arbitrary"),
        name="norm_in_proj",
    )(x, gain, scale, shift, w)


def _rwkv_kernel(ua_ref, shift0_ref, s0_ref, mu_ref, wl_ref, w0_ref, a0_ref, kk_ref, ka_ref, rk_ref,
                 lnw_ref, lnb_ref, sel_ref, y_ref, sout_ref, state_scr, prev_scr, yh_scr, *, chunk, n_valid):
    c = pl.program_id(1)
    n_chunks = pl.num_programs(1)
    C = chunk
    W = A_WIDTH

    @pl.when(c == 0)
    def _():
        state_scr[...] = s0_ref[...]
        prev_scr[...] = shift0_ref[...]

    ua = ua_ref[...]
    row = lax.broadcasted_iota(jnp.int32, (C, 1), 0)
    prev = jnp.where(row == 0, prev_scr[...], pltpu.roll(ua, 1, axis=0))
    prev_scr[...] = ua[C - 1:C, :]
    us = ua + (prev - ua) * mu_ref[...]
    r = us[:, 0:W]
    kr = us[:, W:2 * W]
    vr = us[:, 2 * W:3 * W]
    lin = us[:, 3 * W:3 * W + LORA_IN]
    lane = lax.broadcasted_iota(jnp.int32, (1, LORA_IN), 1)
    act = jnp.where(lane < DECAY_LORA, jnp.tanh(lin),
                    jnp.where(lane < DECAY_LORA + ICLR_LORA, lin, _sigmoid(lin)))
    lora = _dot(act.astype(BF16), wl_ref[...])
    z = -(w0_ref[...] + lora[:, 0:W])
    softplus = jnp.maximum(z, 0.0) + jnp.log(1.0 + jnp.exp(-jnp.abs(z)))
    ld = -jnp.exp(-softplus - 0.5)
    iclr = _sigmoid(a0_ref[...] + lora[:, W:2 * W])
    gate_o = lora[:, 2 * W:3 * W]

    sel = sel_ref[...]
    kk = kr * kk_ref[...]
    kk = kk / jnp.maximum(jnp.sqrt(_group_sum(kk * kk, sel)), 1e-12)
    k_mod = kr * (1.0 + (iclr - 1.0) * ka_ref[...])
    if n_valid is not None:
        valid = (c * C + row) < n_valid
        ld = jnp.where(valid, ld, 0.0)
        kk = jnp.where(valid, kk, 0.0)
        k_mod = jnp.where(valid, k_mod, 0.0)
        vr = jnp.where(valid, vr, 0.0)

    ri = lax.broadcasted_iota(jnp.int32, (C, C), 0)
    ci = lax.broadcasted_iota(jnp.int32, (C, C), 1)
    tri_incl = (ci <= ri)
    tri_strict = (ci < ri)
    cum = _dot_sel_lhs(tri_incl.astype(BF16), ld)
    g_inv = jnp.exp(-cum)
    g_last = jnp.exp(cum[C - 1:C, :])
    at = (-kk * jnp.exp(cum - ld)).astype(BF16)
    bt_f = kk * iclr * g_inv
    kt_f = k_mod * g_inv
    bt = bt_f.astype(BF16)
    kt = kt_f.astype(BF16)
    bl = (bt_f * g_last).astype(BF16)
    kl = (kt_f * g_last).astype(BF16)
    rt = (r * jnp.exp(cum)).astype(BF16)
    vb = vr.astype(BF16)
    eye = (ci == ri).astype(F32)
    n_double = max(int(math.ceil(math.log2(C))) - 1, 0)

    for h in range(A_HEADS):
        hs = slice(h * A_HEAD_DIM, (h + 1) * A_HEAD_DIM)
        s0 = state_scr[h]
        s0b = s0.astype(BF16)
        a_m = jnp.where(tri_strict, _dot_nt(at[:, hs], bt[:, hs]), 0.0)
        b_m = jnp.where(tri_strict, _dot_nt(at[:, hs], kt[:, hs]), 0.0)
        t_inv = eye + a_m
        p = a_m
        for _ in range(n_double):
            pb = p.astype(BF16)
            p = _dot(pb, pb)
            t_inv = t_inv + _dot(t_inv.astype(BF16), p.astype(BF16))
        rhs = _dot_nt(at[:, hs], s0b) + _dot(b_m.astype(BF16), vb[:, hs])
        u = _dot(t_inv.astype(BF16), rhs.astype(BF16))
        ub = u.astype(BF16)
        rb_m = jnp.where(tri_incl, _dot_nt(rt[:, hs], bt[:, hs]), 0.0)
        rk_m = jnp.where(tri_incl, _dot_nt(rt[:, hs], kt[:, hs]), 0.0)
        y = _dot_nt(rt[:, hs], s0b) + _dot(rb_m.astype(BF16), ub) + _dot(rk_m.astype(BF16), vb[:, hs])
        yh_scr[:, hs] = y
        state_scr[h] = s0 * g_last[:, hs] + _dot_tn(ub, bl[:, hs]) + _dot_tn(vb[:, hs], kl[:, hs])

    y = yh_scr[...]
    inv_n = 1.0 / A_HEAD_DIM
    mu_y = _group_sum(y, sel) * inv_n
    dy = y - mu_y
    var = _group_sum(dy * dy, sel) * inv_n
    yn = dy * lax.rsqrt(var + GN_EPS) * lnw_ref[...] + lnb_ref[...]
    bonus = _group_sum(r * k_mod * rk_ref[...], sel) * vr
    y_ref[...] = ((yn + bonus) * gate_o).astype(y_ref.dtype)

    @pl.when(c == n_chunks - 1)
    def _():
        sout_ref[...] = state_scr[...]


def _rwkv(ua, shift0, s0, p, chunk, n_valid):
    b, t, _ = ua.shape
    n_chunks = t // chunk
    vec = lambda n: pl.BlockSpec((1, n), lambda i, j: (0, 0))
    kern = functools.partial(_rwkv_kernel, chunk=chunk, n_valid=n_valid)
    return pl.pallas_call(
        kern,
        grid=(b, n_chunks),
        in_specs=[pl.BlockSpec((None, chunk, A_COLS), lambda i, j: (i, j, 0)),
                  pl.BlockSpec((None, 1, A_COLS), lambda i, j: (i, 0, 0)),
                  pl.BlockSpec((None, A_HEADS, A_HEAD_DIM, A_HEAD_DIM), lambda i, j: (i, 0, 0, 0)),
                  vec(A_COLS),
                  pl.BlockSpec((LORA_IN, 3 * A_WIDTH), lambda i, j: (0, 0)),
                  vec(A_WIDTH), vec(A_WIDTH), vec(A_WIDTH), vec(A_WIDTH), vec(A_WIDTH), vec(A_WIDTH), vec(A_WIDTH),
                  pl.BlockSpec((LANES, LANES), lambda i, j: (0, 0))],
        out_specs=[pl.BlockSpec((None, chunk, A_WIDTH), lambda i, j: (i, j, 0)),
                   pl.BlockSpec((None, A_HEADS, A_HEAD_DIM, A_HEAD_DIM), lambda i, j: (i, 0, 0, 0))],
        out_shape=[jax.ShapeDtypeStruct((b, t, A_WIDTH), BF16),
                   jax.ShapeDtypeStruct((b, A_HEADS, A_HEAD_DIM, A_HEAD_DIM), F32)],
        scratch_shapes=[pltpu.VMEM((A_HEADS, A_HEAD_DIM, A_HEAD_DIM), F32),
                        pltpu.VMEM((1, A_COLS), F32),
                        pltpu.VMEM((chunk, A_WIDTH), F32)],
        compiler_params=_cparams("arbitrary", "arbitrary"),
        name="rwkv7_chunked",
    )(ua, shift0, s0, p["mu_shift"], p["w_lora"], p["w0"], p["a0"], p["k_k"], p["k_a"], p["r_k"],
      p["ln_x_w"], p["ln_x_b"], p["sel64"])


def _qk_prep_kernel(u_ref, qg_ref, kg_ref, sel_ref, q_ref, kf_ref, kb_ref, vf_ref, vb_ref):
    sel = sel_ref[...]
    wq = B_QK_WIDTH
    uq = u_ref[:, 0:wq]
    uk = u_ref[:, wq:2 * wq]
    uv = u_ref[:, 2 * wq:2 * wq + B_V_WIDTH]
    inv_n = 1.0 / B_QK_DIM
    q = uq * lax.rsqrt(_group_sum(uq * uq, sel) * inv_n + NORM_EPS) * qg_ref[...]
    k = uk * lax.rsqrt(_group_sum(uk * uk, sel) * inv_n + NORM_EPS) * kg_ref[...]
    q = q * (B_QK_DIM ** -0.5)
    lane = lax.broadcasted_iota(jnp.int32, (1, wq), 1)
    first_map = (lane % (2 * B_QK_DIM)) < B_QK_DIM
    q_ref[0] = jnp.where(first_map, q, 0.0).astype(BF16)
    q_ref[1] = jnp.where(first_map, 0.0, q).astype(BF16)
    kf_ref[...] = k
    kb_ref[...] = k.astype(BF16)
    vf_ref[...] = uv
    vb_ref[...] = uv.astype(BF16)


def _qk_prep(u_qkv, qg, kg, sel, tm):
    m = u_qkv.shape[0]
    w = B_QK_WIDTH
    row = lambda n: pl.BlockSpec((tm, n), lambda i: (i, 0))
    return pl.pallas_call(
        _qk_prep_kernel,
        grid=(m // tm,),
        in_specs=[row(3 * w),
                  pl.BlockSpec((1, w), lambda i: (0, 0)), pl.BlockSpec((1, w), lambda i: (0, 0)),
                  pl.BlockSpec((LANES, LANES), lambda i: (0, 0))],
        out_specs=[pl.BlockSpec((2, tm, w), lambda i: (0, i, 0)), row(w), row(w), row(w), row(w)],
        out_shape=[jax.ShapeDtypeStruct((2, m, w), BF16), jax.ShapeDtypeStruct((m, w), F32),
                   jax.ShapeDtypeStruct((m, w), BF16), jax.ShapeDtypeStruct((m, w), F32),
                   jax.ShapeDtypeStruct((m, w), BF16)],
        compiler_params=_cparams("arbitrary"),
        name="qk_norm",
    )(u_qkv, qg, kg, sel)


def _lambda(lq1_ref, lk1_ref, lq2_ref, lk2_ref, lam_init):
    s1 = jnp.sum(lq1_ref[...] * lk1_ref[...], axis=-1, keepdims=True)
    s2 = jnp.sum(lq2_ref[...] * lk2_ref[...], axis=-1, keepdims=True)
    return jnp.exp(s1) - jnp.exp(s2) + lam_init


def _sub_norm(o, subln, lam_init):
    ms = jnp.mean(o * o, axis=-1, keepdims=True)
    return o * lax.rsqrt(ms + NORM_EPS) * subln * (1.0 - lam_init)


def _flash_kernel(rb_ref, q_ref, k_ref, v_ref, lq1_ref, lk1_ref, lq2_ref, lk2_ref, subln_ref, o_ref,
                  m_scr, l_scr, acc_scr, bias_scr, *, t, lam_init):
    h = pl.program_id(0)
    i = pl.program_id(1)
    j = pl.program_id(2)

    @pl.when((i == 0) & (j == 0))
    def _():
        r = lax.broadcasted_iota(jnp.int32, (t, t), 0)
        c = lax.broadcasted_iota(jnp.int32, (t, t), 1)
        table = lambda b: rb_ref[b, h]
        bias_scr[0] = jnp.where(r >= c, _rel_bias_minus_far(r - c, table), NEG_INF)
        bias_scr[1] = _rel_bias_minus_far(r - c + t, table)

    @pl.when(j == 0)
    def _():
        m_scr[...] = jnp.full(m_scr.shape, NEG_INF, F32)
        l_scr[...] = jnp.zeros(l_scr.shape, F32)
        acc_scr[...] = jnp.zeros(acc_scr.shape, F32)

    def step(bias):
        q = q_ref[...].reshape(2 * t, 2 * B_QK_DIM)
        s = _dot_nt(q, k_ref[...])
        if bias is not None:
            s = (s.reshape(2, t, t) + bias[None]).reshape(2 * t, t)
        m_prev = m_scr[...]
        m_new = jnp.maximum(m_prev, jnp.max(s, axis=-1, keepdims=True))
        alpha = jnp.exp(m_prev - m_new)
        p = jnp.exp(s - m_new)
        l_scr[...] = alpha * l_scr[...] + jnp.sum(p, axis=-1, keepdims=True)
        acc_scr[...] = alpha * acc_scr[...] + _dot(p.astype(BF16), v_ref[...])
        m_scr[...] = m_new

    @pl.when(j < i - 1)
    def _():
        step(None)

    @pl.when((j == i - 1) & (i > 0))
    def _():
        step(bias_scr[1])

    @pl.when(j == i)
    def _():
        step(bias_scr[0])
        lam = _lambda(lq1_ref, lk1_ref, lq2_ref, lk2_ref, lam_init)
        o = acc_scr[...] / l_scr[...]
        o = o[0:t] - lam * o[t:2 * t]
        o_ref[...] = _sub_norm(o, subln_ref[...], lam_init).astype(o_ref.dtype)


def _flash_prompt(q2, kb, vb, rel_bias, lam_p, subln, lam_init, t):
    n_tok = kb.shape[0]
    nb = n_tok // t
    hd = 2 * B_QK_DIM
    vec = lambda n: pl.BlockSpec((1, n), lambda h, i, j: (0, 0))
    kern = functools.partial(_flash_kernel, t=t, lam_init=lam_init)
    return pl.pallas_call(
        kern,
        grid=(B_HEADS, nb, nb),
        in_specs=[pl.BlockSpec(memory_space=pltpu.SMEM),
                  pl.BlockSpec((2, t, hd), lambda h, i, j: (0, i, h)),
                  pl.BlockSpec((t, hd), lambda h, i, j: (jnp.minimum(j, i), h)),
                  pl.BlockSpec((t, B_V_DIM), lambda h, i, j: (jnp.minimum(j, i), h)),
                  vec(B_QK_DIM), vec(B_QK_DIM), vec(B_QK_DIM), vec(B_QK_DIM), vec(B_V_DIM)],
        out_specs=pl.BlockSpec((t, B_V_DIM), lambda h, i, j: (i, h)),
        out_shape=jax.ShapeDtypeStruct((n_tok, B_V_WIDTH), BF16),
        scratch_shapes=[pltpu.VMEM((2 * t, 1), F32), pltpu.VMEM((2 * t, 1), F32),
                        pltpu.VMEM((2 * t, B_V_DIM), F32), pltpu.VMEM((2, t, t), F32)],
        compiler_params=_cparams("arbitrary", "arbitrary", "arbitrary"),
        name="diff_flash_attention",
    )(rel_bias, q2, kb, vb, *lam_p, subln)


def _decode_kernel(pt_ref, q_ref, kn_ref, vn_ref, rbrow_ref, lq1_ref, lk1_ref, lq2_ref, lk2_ref, subln_ref,
                   *rest, pages_per_step, page_size, n_new, lam_init):
    P = pages_per_step
    k_refs = rest[0:P]
    v_refs = rest[P:2 * P]
    o_ref = rest[2 * P]
    m_scr, l_scr, acc_scr = rest[2 * P + 1:]
    j = pl.program_id(1)
    last = pl.num_programs(1) - 1
    R = q_ref.shape[0]
    rows_per_map = B_HEADS * n_new
    L = page_size * B_HEADS

    @pl.when(j == 0)
    def _():
        m_scr[...] = jnp.full(m_scr.shape, NEG_INF, F32)
        l_scr[...] = jnp.zeros(l_scr.shape, F32)
        acc_scr[...] = jnp.zeros(acc_scr.shape, F32)

    q = q_ref[...]

    def row_head_query(width):
        r = lax.broadcasted_iota(jnp.int32, (R, width), 0) % rows_per_map
        return r // n_new, r % n_new

    def online(s, v):
        m_prev = m_scr[...]
        m_new = jnp.maximum(m_prev, jnp.max(s, axis=-1, keepdims=True))
        alpha = jnp.exp(m_prev - m_new)
        p = jnp.exp(s - m_new)
        l_scr[...] = alpha * l_scr[...] + jnp.sum(p, axis=-1, keepdims=True)
        acc_scr[...] = alpha * acc_scr[...] + _dot(p.astype(BF16), v)
        m_scr[...] = m_new

    def page(pi, with_bias):
        k2 = k_refs[pi][...].reshape(L, 2 * B_QK_DIM).astype(BF16)
        v2 = v_refs[pi][...].reshape(L, B_V_DIM).astype(BF16)
        s = _dot_nt(q, k2)
        col = lax.broadcasted_iota(jnp.int32, (R, L), 1)
        head, qi = row_head_query(L)
        if with_bias:
            dist = qi + page_size - col // B_HEADS
            s = s + _rel_bias_minus_far(dist, lambda b: rbrow_ref[:, b:b + 1])
        s = jnp.where(col % B_HEADS == head, s, NEG_INF)
        online(s, v2)

    for pi in range(P - 1):
        page(pi, False)

    @pl.when(j != last)
    def _():
        page(P - 1, False)

    @pl.when(j == last)
    def _():
        page(P - 1, True)
        kn = kn_ref[...]
        s = _dot_nt(q, kn)
        nl = kn.shape[0]
        col = lax.broadcasted_iota(jnp.int32, (R, nl), 1)
        head, qi = row_head_query(nl)
        tok = col // B_HEADS
        s = s + _rel_bias_minus_far(qi - tok, lambda b: rbrow_ref[:, b:b + 1])
        s = jnp.where((col % B_HEADS == head) & (tok <= qi), s, NEG_INF)
        online(s, vn_ref[...])
        lam = _lambda(lq1_ref, lk1_ref, lq2_ref, lk2_ref, lam_init)
        o = acc_scr[...] / l_scr[...]
        o = o[0:rows_per_map] - lam * o[rows_per_map:2 * rows_per_map]
        o_ref[...] = _sub_norm(o, subln_ref[...], lam_init).astype(o_ref.dtype)


def _decode_attention(page_table, qd, kn, vn, rbrow, lam_p, subln, cache_k, cache_v, layer, lam_init, n_new,
                      pages_per_step):
    nseq, n_pages = page_table.shape
    page_size = cache_k.shape[2]
    P = pages_per_step
    n_steps = n_pages // P
    R = qd.shape[1]
    hd = 2 * B_QK_DIM
    vec = lambda n: pl.BlockSpec((1, n), lambda b, j, pt: (0, 0))

    def page_spec(pi, width):
        return pl.BlockSpec((None, None, page_size, B_HEADS, width),
                            lambda b, j, pt: (layer, pt[b, j * P + pi], 0, 0, 0))

    kern = functools.partial(_decode_kernel, pages_per_step=P, page_size=page_size, n_new=n_new, lam_init=lam_init)
    grid_spec = pltpu.PrefetchScalarGridSpec(
        num_scalar_prefetch=1,
        grid=(nseq, n_steps),
        in_specs=[pl.BlockSpec((None, R, hd), lambda b, j, pt: (b, 0, 0)),
                  pl.BlockSpec((None, kn.shape[1], hd), lambda b, j, pt: (b, 0, 0)),
                  pl.BlockSpec((None, vn.shape[1], B_V_DIM), lambda b, j, pt: (b, 0, 0)),
                  pl.BlockSpec(rbrow.shape, lambda b, j, pt: (0, 0)),
                  vec(B_QK_DIM), vec(B_QK_DIM), vec(B_QK_DIM), vec(B_QK_DIM), vec(B_V_DIM)]
                 + [page_spec(pi, hd) for pi in range(P)] + [page_spec(pi, B_V_DIM) for pi in range(P)],
        out_specs=pl.BlockSpec((None, R // 2, B_V_DIM), lambda b, j, pt: (b, 0, 0)),
        scratch_shapes=[pltpu.VMEM((R, 1), F32), pltpu.VMEM((R, 1), F32), pltpu.VMEM((R, B_V_DIM), F32)],
    )
    return pl.pallas_call(
        kern,
        grid_spec=grid_spec,
        out_shape=jax.ShapeDtypeStruct((nseq, R // 2, B_V_DIM), BF16),
        compiler_params=_cparams("arbitrary", "arbitrary"),
        name="diff_paged_decode",
    )(page_table, qd, kn, vn, rbrow, *lam_p, subln, *([cache_k] * P), *([cache_v] * P))


def _merge_kernel(ya_ref, yb_ref, ga_ref, gb_ref, wa_ref, wb_ref, o_ref):
    ma = _dot(ya_ref[...], wa_ref[...])
    mb = _dot(yb_ref[...], wb_ref[...])
    o_ref[...] = (_sigmoid(ga_ref[...]) * ma + _sigmoid(gb_ref[...]) * mb).astype(o_ref.dtype)


def _merge(ya, yb, gates, wa, wb, tm):
    m = ya.shape[0]
    d = wa.shape[1]
    return pl.pallas_call(
        _merge_kernel,
        grid=(m // tm,),
        in_specs=[pl.BlockSpec((tm, ya.shape[1]), lambda i: (i, 0)),
                  pl.BlockSpec((tm, yb.shape[1]), lambda i: (i, 0)),
                  pl.BlockSpec((tm, d), lambda i: (i, 0)),
                  pl.BlockSpec((tm, d), lambda i: (i, 1)),
                  pl.BlockSpec(wa.shape, lambda i: (0, 0)),
                  pl.BlockSpec(wb.shape, lambda i: (0, 0))],
        out_specs=pl.BlockSpec((tm, d), lambda i: (i, 0)),
        out_shape=jax.ShapeDtypeStruct((m, d), BF16),
        compiler_params=_cparams("arbitrary"),
        name="branch_merge",
    )(ya, yb, gates, gates, wa, wb)


def _out_router_kernel(mg_ref, x_ref, g1_ref, gain_ref, scale_ref, shift_ref, wo_ref, wrh_ref, wrm_ref, br_ref,
                       x1_ref, h2_ref, comb_ref):
    x1 = x_ref[...] + g1_ref[...] * _dot(mg_ref[...], wo_ref[...])
    x1_ref[...] = x1
    h2 = _modnorm(x1, gain_ref[...], scale_ref[...], shift_ref[...])
    h2_ref[...] = h2.astype(BF16)
    hi, mid, _ = _split3(h2)
    logit = _dot(hi, wrh_ref[...]) + _dot(hi, wrm_ref[...]) + _dot(mid, wrh_ref[...]) + br_ref[...]
    lane = lax.broadcasted_iota(jnp.int32, logit.shape, 1)
    big = jnp.int32(2 ** 30)

    def first_max(mask):
        v = jnp.max(jnp.where(mask, logit, NEG_INF), axis=-1, keepdims=True)
        idx = jnp.min(jnp.where(mask & (logit == v), lane, big), axis=-1, keepdims=True)
        return v, idx

    coarse = lane < N_GROUPS
    vc, g_idx = first_max(coarse)
    p_g = 1.0 / jnp.sum(jnp.where(coarse, jnp.exp(logit - vc), 0.0), axis=-1, keepdims=True)
    lo = N_GROUPS + g_idx * EXPERTS_PER_GROUP
    fine = (lane >= lo) & (lane < lo + EXPERTS_PER_GROUP)
    v1, i1 = first_max(fine)
    v2, i2 = first_max(fine & (lane != i1))
    e2 = jnp.exp(v2 - v1)
    p1 = 1.0 / (1.0 + e2)
    comb_ref[...] = jnp.where(lane == i1, p_g * p1, jnp.where(lane == i2, p_g * (e2 * p1), 0.0))


def _out_router(mg, x, g1, gain, scale, shift, wo, wrh, wrm, br, tm):
    m, d = x.shape
    full = lambda a: pl.BlockSpec(a.shape, lambda i: (0, 0))
    row = lambda n: pl.BlockSpec((tm, n), lambda i: (i, 0))
    return pl.pallas_call(
        _out_router_kernel,
        grid=(m // tm,),
        in_specs=[row(d), row(d), _mod_spec(g1, tm), full(gain), _mod_spec(scale, tm), _mod_spec(shift, tm),
                  full(wo), full(wrh), full(wrm), full(br)],
        out_specs=[row(d), row(d), row(ROUTER_LANES)],
        out_shape=[jax.ShapeDtypeStruct((m, d), F32), jax.ShapeDtypeStruct((m, d), BF16),
                   jax.ShapeDtypeStruct((m, ROUTER_LANES), F32)],
        compiler_params=_cparams("arbitrary"),
        name="out_proj_router",
    )(mg, x, g1, gain, scale, shift, wo, wrh, wrm, br)


def _moe_kernel(h_ref, comb_ref, x1_ref, g2_ref, wg_ref, wu_ref, wd_ref, o_ref, acc_scr):
    e = pl.program_id(1)

    @pl.when(e == 0)
    def _():
        acc_scr[...] = jnp.zeros(acc_scr.shape, F32)

    h = h_ref[...]
    hg = _dot(h, wg_ref[...])
    hu = _dot(h, wu_ref[...])
    lane = lax.broadcasted_iota(jnp.int32, comb_ref.shape, 1)
    w = jnp.sum(jnp.where(lane == e + N_GROUPS, comb_ref[...], 0.0), axis=-1, keepdims=True)
    act = _silu(hg) * hu * w
    acc_scr[...] += _dot(act.astype(BF16), wd_ref[...])

    @pl.when(e == pl.num_programs(1) - 1)
    def _():
        o_ref[...] = x1_ref[...] + g2_ref[...] * acc_scr[...]


def _moe(h2, comb, x1, g2, wg, wu, wd, tm):
    m, d = x1.shape
    ne, _, f = wg.shape
    row = lambda n: pl.BlockSpec((tm, n), lambda i, e: (i, 0))
    return pl.pallas_call(
        _moe_kernel,
        grid=(m // tm, ne),
        in_specs=[row(d), row(ROUTER_LANES), row(d), _mod_spec(g2, tm),
                  pl.BlockSpec((None, d, f), lambda i, e: (e, 0, 0)),
                  pl.BlockSpec((None, d, f), lambda i, e: (e, 0, 0)),
                  pl.BlockSpec((None, f, d), lambda i, e: (e, 0, 0))],
        out_specs=row(d),
        out_shape=jax.ShapeDtypeStruct((m, d), F32),
        scratch_shapes=[pltpu.VMEM((tm, d), F32)],
        compiler_params=_cparams("arbitrary", "arbitrary"),
        name="moe_experts",
    )(h2, comb, x1, g2, wg, wu, wd)


def _pick(m, prefs):
    for t in prefs:
        if m % t == 0:
            return t
    return m


def _prep_layer_params(l, w_in, mu_shift, w0, w_decay_up, a0, w_iclr_up, w_gate_up, k_k, k_a, r_k, ln_x_w, ln_x_b,
                       q_norm, k_norm, lambda_q1, lambda_k1, lambda_q2, lambda_k2, subln, w_branch_a, w_branch_b,
                       w_out, w_router_coarse, b_router_coarse, w_router_fine, b_router_fine,
                       w_exp_gate, w_exp_up, w_exp_down, norm1, norm2):
    d = w_in.shape[1]
    wi = w_in[l]
    c_qkv = A_COLS + 2 * B_QK_WIDTH + B_V_WIDTH
    w_lora = jnp.zeros((LORA_IN, 3 * A_WIDTH), F32)
    w_lora = w_lora.at[0:DECAY_LORA, 0:A_WIDTH].set(w_decay_up[l])
    w_lora = w_lora.at[DECAY_LORA:DECAY_LORA + ICLR_LORA, A_WIDTH:2 * A_WIDTH].set(w_iclr_up[l])
    w_lora = w_lora.at[DECAY_LORA + ICLR_LORA:, 2 * A_WIDTH:].set(w_gate_up[l])
    lane = jnp.arange(LANES)
    sel64 = (lane[:, None] // A_HEAD_DIM == lane[None, :] // A_HEAD_DIM).astype(BF16)
    w_r = jnp.zeros((d, ROUTER_LANES), F32)
    w_r = w_r.at[:, 0:N_GROUPS].set(w_router_coarse[l])
    w_r = w_r.at[:, N_GROUPS:N_GROUPS + N_EXPERTS].set(
        jnp.transpose(w_router_fine[l], (1, 0, 2)).reshape(d, N_EXPERTS))
    w_rh = w_r.astype(BF16)
    w_rm = (w_r - w_rh.astype(F32)).astype(BF16)
    b_r = jnp.zeros((1, ROUTER_LANES), F32)
    b_r = b_r.at[0, 0:N_GROUPS].set(b_router_coarse[l])
    b_r = b_r.at[0, N_GROUPS:N_GROUPS + N_EXPERTS].set(b_router_fine[l].reshape(-1))
    row = lambda a: a.reshape(1, -1)
    return dict(
        w_in_a=wi[:, 0:A_COLS].astype(BF16), w_in_qkv=wi[:, A_COLS:c_qkv].astype(BF16),
        w_in_g=wi[:, c_qkv:].astype(BF16),
        mu_shift=row(mu_shift[l]), w_lora=w_lora.astype(BF16), w0=row(w0[l]), a0=row(a0[l]), k_k=row(k_k[l]),
        k_a=row(k_a[l]), r_k=row(r_k[l]), ln_x_w=row(ln_x_w[l]), ln_x_b=row(ln_x_b[l]), sel64=sel64,
        q_gain=row(jnp.tile(q_norm[l].reshape(-1), B_HEADS)), k_gain=row(jnp.tile(k_norm[l].reshape(-1), B_HEADS)),
        lam=(row(lambda_q1[l]), row(lambda_k1[l]), row(lambda_q2[l]), row(lambda_k2[l])), subln=row(subln[l]),
        w_a=w_branch_a[l].astype(BF16), w_b=w_branch_b[l].astype(BF16), w_out=w_out[l].astype(BF16),
        w_rh=w_rh, w_rm=w_rm, b_r=b_r,
        w_eg=w_exp_gate[l].astype(BF16), w_eu=w_exp_up[l].astype(BF16), w_ed=w_exp_down[l].astype(BF16),
        norm1=row(norm1[l]), norm2=row(norm2[l]))


def _trunk_rows(x2, mods, p, tm):
    sh1, sc1 = mods[0], mods[1]
    tm_a = _pick(x2.shape[0], (tm,))
    ua = _norm_proj(x2, p["norm1"], sc1, sh1, p["w_in_a"], tm_a, A_COLS // 2)
    uqkv = _norm_proj(x2, p["norm1"], sc1, sh1, p["w_in_qkv"], tm_a, 1024)
    ug = _norm_proj(x2, p["norm1"], sc1, sh1, p["w_in_g"], tm_a, 1024)
    return ua, uqkv, ug


def _tail_rows(x2, ya, yb, ug, mods, p, tm):
    _, _, g1, sh2, sc2, g2 = mods
    m = x2.shape[0]
    mg = _merge(ya, yb, ug, p["w_a"], p["w_b"], _pick(m, (tm,)))
    x1, h2, comb = _out_router(mg, x2, g1, p["norm2"], sc2, sh2, p["w_out"], p["w_rh"], p["w_rm"], p["b_r"],
                               _pick(m, (256, 128)))
    return _moe(h2, comb, x1, g2, p["w_eg"], p["w_eu"], p["w_ed"], _pick(m, (tm,)))


def kernel(x_prompt, x_sample, c_prompt, c_sample, cache_k, cache_v, state_wkv, state_shift, page_table, rel_bias, w_ada, b_ada, norm1, norm2, w_in, mu_shift, w0, w_decay_up, a0, w_iclr_up, w_gate_up, k_k, k_a, r_k, ln_x_w, ln_x_b, q_norm, k_norm, lambda_q1, lambda_k1, lambda_q2, lambda_k2, subln, w_branch_a, w_branch_b, w_out, w_router_coarse, b_router_coarse, w_router_fine, b_router_fine, w_exp_gate, w_exp_up, w_exp_down):
    depth = w_in.shape[0]
    bp, tp, d = x_prompt.shape
    bs, ts, _ = x_sample.shape
    n_new_pad = 8
    kv_new_rows = LANES // B_HEADS
    yp = x_prompt.reshape(bp * tp, d)
    ys = x_sample.reshape(bs * ts, d)
    c_all = jnp.concatenate([c_prompt, c_sample], axis=0)
    c_rows = -(-c_all.shape[0] // 8) * 8
    c_all = jnp.pad(c_all, ((0, c_rows - c_all.shape[0]), (0, 0)))
    outs = [[] for _ in range(8)]
    for l in range(depth):
        lam_init = 0.8 - 0.6 * math.exp(-0.3 * l)
        p = _prep_layer_params(l, w_in, mu_shift, w0, w_decay_up, a0, w_iclr_up, w_gate_up, k_k, k_a, r_k, ln_x_w,
                               ln_x_b, q_norm, k_norm, lambda_q1, lambda_k1, lambda_q2, lambda_k2, subln,
                               w_branch_a, w_branch_b, w_out, w_router_coarse, b_router_coarse, w_router_fine,
                               b_router_fine, w_exp_gate, w_exp_up, w_exp_down, norm1, norm2)
        ada = _ada(c_all, w_ada[l], b_ada[l])
        mods_p = [jnp.repeat(a, tp, axis=0) if bp > 1 else a for a in jnp.split(ada[0:bp], 6, axis=-1)]
        ua, uqkv, ug = _trunk_rows(yp, mods_p, p, 512)
        ua3 = ua.reshape(bp, tp, A_COLS)
        ya, wkv_p = _rwkv(ua3, jnp.zeros((bp, 1, A_COLS), F32),
                          jnp.zeros((bp, A_HEADS, A_HEAD_DIM, A_HEAD_DIM), F32), p, 64, None)
        q2, kf, kb, vf, vb = _qk_prep(uqkv, p["q_gain"], p["k_gain"], p["sel64"], 512)
        t_blk = _pick(tp, (512, 256, 128))
        yb = jnp.concatenate(
            [_flash_prompt(q2[:, b * tp:(b + 1) * tp], kb[b * tp:(b + 1) * tp], vb[b * tp:(b + 1) * tp],
                           rel_bias, p["lam"], p["subln"], lam_init, t_blk) for b in range(bp)], axis=0)
        yp = _tail_rows(yp, ya.reshape(bp * tp, A_WIDTH), yb, ug, mods_p, p, 512)
        outs[0].append(kf.reshape(bp, tp, B_HEADS, 2 * B_QK_DIM))
        outs[1].append(vf.reshape(bp, tp, B_HEADS, B_V_DIM))
        outs[2].append(wkv_p)
        outs[3].append(ua3[:, -1])
        mods_s = [jnp.repeat(a, ts, axis=0) for a in jnp.split(ada[bp:bp + bs], 6, axis=-1)]
        ua, uqkv, ug = _trunk_rows(ys, mods_s, p, 128)
        ua3 = ua.reshape(bs, ts, A_COLS)
        ua_pad = jnp.pad(ua3, ((0, 0), (0, n_new_pad - ts), (0, 0)))
        ya, wkv_s = _rwkv(ua_pad, state_shift[l][:, None, :], state_wkv[l], p, n_new_pad, ts)
        ya = ya[:, 0:ts].reshape(bs * ts, A_WIDTH)
        q2, kf, kb, vf, vb = _qk_prep(uqkv, p["q_gain"], p["k_gain"], p["sel64"], _pick(bs * ts, (128,)))
        qd = q2.reshape(2, bs, ts, B_HEADS, 2 * B_QK_DIM).transpose(1, 0, 3, 2, 4).reshape(
            bs, 2 * B_HEADS * ts, 2 * B_QK_DIM)
        kn = jnp.pad(kb.reshape(bs, ts * B_HEADS, 2 * B_QK_DIM), ((0, 0), (0, (kv_new_rows - ts) * B_HEADS), (0, 0)))
        vn = jnp.pad(vb.reshape(bs, ts * B_HEADS, B_V_DIM), ((0, 0), (0, (kv_new_rows - ts) * B_HEADS), (0, 0)))
        rbrow = jnp.tile(jnp.repeat(rel_bias.T, ts, axis=0), (2, 1))
        od = _decode_attention(page_table, qd, kn, vn, rbrow, p["lam"], p["subln"], cache_k, cache_v, l, lam_init,
                               ts, _pick(page_table.shape[1], (4, 2, 1)))
        yb = od.reshape(bs, B_HEADS, ts, B_V_DIM).transpose(0, 2, 1, 3).reshape(bs * ts, B_V_WIDTH)
        ys = _tail_rows(ys, ya, yb, ug, mods_s, p, 128)
        outs[4].append(kf.reshape(bs, ts, B_HEADS, 2 * B_QK_DIM))
        outs[5].append(vf.reshape(bs, ts, B_HEADS, B_V_DIM))
        outs[6].append(wkv_s)
        outs[7].append(ua3[:, -1])
    st = [jnp.stack(o) for o in outs]
    return (yp.reshape(bp, tp, d), ys.reshape(bs, ts, d), st[0], st[1], st[2], st[3], st[4], st[5], st[6], st[7])
```

```python
import functools
import math

import jax
import jax.numpy as jnp
from jax import lax
from jax.experimental import pallas as pl
from jax.experimental.pallas import tpu as pltpu

F32 = jnp.float32
BF16 = jnp.bfloat16

A_HEADS = 16
A_HEAD_DIM = 64
A_WIDTH = A_HEADS * A_HEAD_DIM
DECAY_LORA = 64
ICLR_LORA = 64
GATE_LORA = 128
LORA_IN = DECAY_LORA + ICLR_LORA + GATE_LORA
A_COLS = 3 * A_WIDTH + LORA_IN
GN_EPS = 64e-5
B_HEADS = 8
B_QK_DIM = 64
B_V_DIM = 2 * B_QK_DIM
B_QK_WIDTH = B_HEADS * 2 * B_QK_DIM
B_V_WIDTH = B_HEADS * B_V_DIM
NUM_BUCKETS = 32
MAX_DISTANCE = 128
N_GROUPS = 4
EXPERTS_PER_GROUP = 8
N_EXPERTS = N_GROUPS * EXPERTS_PER_GROUP
NORM_EPS = 1e-6
NEG_INF = -1e30

LANES = 128
ROUTER_LANES = LANES
VMEM_LIMIT = 56 * 1024 * 1024


def _cparams(*sem):
    return pltpu.CompilerParams(dimension_semantics=sem, vmem_limit_bytes=VMEM_LIMIT)


def _dot(a, b):
    return jnp.dot(a, b, preferred_element_type=F32)


def _dot_nt(a, b):
    return lax.dot_general(a, b, (((1,), (1,)), ((), ())), preferred_element_type=F32)


def _dot_tn(a, b):
    return lax.dot_general(a, b, (((0,), (0,)), ((), ())), preferred_element_type=F32)


def _split3(x):
    hi = x.astype(BF16)
    r1 = x - hi.astype(F32)
    mid = r1.astype(BF16)
    lo = (r1 - mid.astype(F32)).astype(BF16)
    return hi, mid, lo


def _dot_sel_rhs(x, sel):
    hi, mid, lo = _split3(x)
    return _dot(hi, sel) + _dot(mid, sel) + _dot(lo, sel)


def _dot_sel_lhs(sel, x):
    hi, mid, lo = _split3(x)
    return _dot(sel, hi) + _dot(sel, mid) + _dot(sel, lo)


def _group_sum(x, sel):
    parts = [_dot_sel_rhs(x[:, t * LANES:(t + 1) * LANES], sel) for t in range(x.shape[1] // LANES)]
    return parts[0] if len(parts) == 1 else jnp.concatenate(parts, axis=1)


def _sigmoid(x):
    return 1.0 / (1.0 + jnp.exp(-x))


def _silu(x):
    return x * _sigmoid(x)


def _rel_bias_minus_far(dist, table_fn):
    n = jnp.maximum(dist, 0)
    max_exact = NUM_BUCKETS // 2
    nf = jnp.maximum(n, 1).astype(F32)
    large = max_exact + (jnp.log(nf / max_exact) / math.log(MAX_DISTANCE / max_exact)
                         * (NUM_BUCKETS - max_exact)).astype(jnp.int32)
    large = jnp.minimum(large, NUM_BUCKETS - 1)
    bucket = jnp.where(n < max_exact, n, large)
    far = table_fn(NUM_BUCKETS - 1)
    out = jnp.zeros(dist.shape, F32)
    for b in range(NUM_BUCKETS - 1):
        out = jnp.where(bucket == b, table_fn(b) - far, out)
    return out


def _ada_kernel(c_ref, w_ref, b_ref, o_ref):
    s = _silu(c_ref[...])
    o_ref[...] = _dot(s.astype(BF16), w_ref[...].astype(BF16)) + b_ref[...]


def _ada(c, w, b):
    m, d = c.shape
    n = w.shape[1]
    tn = 1024
    return pl.pallas_call(
        _ada_kernel,
        grid=(n // tn,),
        in_specs=[pl.BlockSpec((m, d), lambda j: (0, 0)),
                  pl.BlockSpec((d, tn), lambda j: (0, j)),
                  pl.BlockSpec((1, tn), lambda j: (0, j))],
        out_specs=pl.BlockSpec((m, tn), lambda j: (0, j)),
        out_shape=jax.ShapeDtypeStruct((m, n), F32),
        compiler_params=_cparams("arbitrary"),
        name="ada_modulation",
    )(c, w, b.reshape(1, n))


def _modnorm(x, gain, scale, shift):
    ms = jnp.mean(x * x, axis=-1, keepdims=True)
    h = x * lax.rsqrt(ms + NORM_EPS) * gain
    return h * (1.0 + scale) + shift


def _norm_proj_kernel(x_ref, gain_ref, scale_ref, shift_ref, w_ref, o_ref, h_scr):
    @pl.when(pl.program_id(1) == 0)
    def _():
        h_scr[...] = _modnorm(x_ref[...], gain_ref[...], scale_ref[...], shift_ref[...]).astype(BF16)

    o_ref[...] = _dot(h_scr[...], w_ref[...])


def _mod_spec(mod, tm):
    if mod.shape[0] == 1:
        return pl.BlockSpec((1, mod.shape[1]), lambda i, *_: (0, 0))
    return pl.BlockSpec((tm, mod.shape[1]), lambda i, *_: (i, 0))


def _norm_proj(x, gain, scale, shift, w, tm, tn):
    m, d = x.shape
    n = w.shape[1]
    return pl.pallas_call(
        _norm_proj_kernel,
        grid=(m // tm, n // tn),
        in_specs=[pl.BlockSpec((tm, d), lambda i, j: (i, 0)),
                  pl.BlockSpec((1, d), lambda i, j: (0, 0)),
                  _mod_spec(scale, tm), _mod_spec(shift, tm),
                  pl.BlockSpec((d, tn), lambda i, j: (0, j))],
        out_specs=pl.BlockSpec((tm, tn), lambda i, j: (i, j)),
        out_shape=jax.ShapeDtypeStruct((m, n), F32),
        scratch_shapes=[pltpu.VMEM((tm, d), BF16)],
        compiler_params=_cparams("arbitrary", "arbitrary"),
        name="norm_in_proj",
    )(x, gain, scale, shift, w)


def _rwkv_kernel(ua_ref, shift0_ref, s0_ref, mu_ref, wl_ref, w0_ref, a0_ref, kk_ref, ka_ref, rk_ref,
                 lnw_ref, lnb_ref, sel_ref, y_ref, sout_ref, state_scr, prev_scr, yh_scr, *, chunk, n_valid):
    c = pl.program_id(1)
    n_chunks = pl.num_programs(1)
    C = chunk
    W = A_WIDTH

    @pl.when(c == 0)
    def _():
        state_scr[...] = s0_ref[...]
        prev_scr[...] = shift0_ref[...]

    ua = ua_ref[...]
    row = lax.broadcasted_iota(jnp.int32, (C, 1), 0)
    prev = jnp.where(row == 0, prev_scr[...], pltpu.roll(ua, 1, axis=0))
    prev_scr[...] = ua[C - 1:C, :]
    us = ua + (prev - ua) * mu_ref[...]
    r = us[:, 0:W]
    kr = us[:, W:2 * W]
    vr = us[:, 2 * W:3 * W]
    lin = us[:, 3 * W:3 * W + LORA_IN]
    lane = lax.broadcasted_iota(jnp.int32, (1, LORA_IN), 1)
    act = jnp.where(lane < DECAY_LORA, jnp.tanh(lin),
                    jnp.where(lane < DECAY_LORA + ICLR_LORA, lin, _sigmoid(lin)))
    lora = _dot(act.astype(BF16), wl_ref[...])
    z = -(w0_ref[...] + lora[:, 0:W])
    softplus = jnp.maximum(z, 0.0) + jnp.log(1.0 + jnp.exp(-jnp.abs(z)))
    ld = -jnp.exp(-softplus - 0.5)
    iclr = _sigmoid(a0_ref[...] + lora[:, W:2 * W])
    gate_o = lora[:, 2 * W:3 * W]

    sel = sel_ref[...]
    kk = kr * kk_ref[...]
    kk = kk / jnp.maximum(jnp.sqrt(_group_sum(kk * kk, sel)), 1e-12)
    k_mod = kr * (1.0 + (iclr - 1.0) * ka_ref[...])
    if n_valid is not None:
        valid = (c * C + row) < n_valid
        ld = jnp.where(valid, ld, 0.0)
        kk = jnp.where(valid, kk, 0.0)
        k_mod = jnp.where(valid, k_mod, 0.0)
        vr = jnp.where(valid, vr, 0.0)

    ri = lax.broadcasted_iota(jnp.int32, (C, C), 0)
    ci = lax.broadcasted_iota(jnp.int32, (C, C), 1)
    cum = _dot_sel_lhs((ci <= ri).astype(BF16), ld)
    g_inv = jnp.exp(-cum)
    g_last = jnp.exp(cum[C - 1:C, :])
    at = -kk * jnp.exp(cum - ld)
    bt = kk * iclr * g_inv
    kt = k_mod * g_inv
    bl = bt * g_last
    kl = kt * g_last
    rt = r * jnp.exp(cum)
    eye = (ci == ri).astype(F32)
    n_double = max(int(math.ceil(math.log2(C))) - 1, 0)
    r2 = lax.broadcasted_iota(jnp.int32, (2 * C, 2 * C), 0)
    c2 = lax.broadcasted_iota(jnp.int32, (2 * C, 2 * C), 1)
    mask2 = ((r2 < C) & ((c2 % C) < r2)) | ((r2 >= C) & ((c2 % C) <= (r2 - C)))
    zeros_c = jnp.zeros((C, A_HEAD_DIM), F32)
    heads = range(A_HEADS)

    def hsl(x, h):
        return x[:, h * A_HEAD_DIM:(h + 1) * A_HEAD_DIM]

    def stack(top, bottom):
        return jnp.concatenate([top, bottom], axis=0).astype(BF16)

    atb = [hsl(at, h).astype(BF16) for h in heads]
    ar = [stack(hsl(at, h), hsl(rt, h)) for h in heads]
    bk = [stack(hsl(bt, h), hsl(kt, h)) for h in heads]
    zv = [stack(zeros_c, hsl(vr, h)) for h in heads]
    bkl = [stack(hsl(bl, h), hsl(kl, h)) for h in heads]
    m2 = [jnp.where(mask2, _dot_nt(ar[h], bk[h]), 0.0) for h in heads]
    m2b = [m.astype(BF16) for m in m2]
    bvk = [_dot(m2b[h], zv[h]) for h in heads]
    p = [m[0:C, 0:C] for m in m2]
    t_inv = [eye + a for a in p]
    for _ in range(n_double):
        pb = [x.astype(BF16) for x in p]
        p = [_dot(x, x) for x in pb]
        t_inv = [t + _dot(t.astype(BF16), x.astype(BF16)) for t, x in zip(t_inv, p)]
    tb = [t.astype(BF16) for t in t_inv]
    w = [_dot(tb[h], atb[h]) for h in heads]
    z = [_dot(tb[h], bvk[h][0:C].astype(BF16)) for h in heads]
    s0 = [state_scr[h] for h in heads]
    ws = [_dot_nt(stack(w[h], hsl(rt, h)), s0[h].astype(BF16)) for h in heads]
    u = [ws[h][0:C] + z[h] for h in heads]
    y = [ws[h][C:2 * C] + bvk[h][C:2 * C] + _dot(m2[h][C:2 * C, 0:C].astype(BF16), u[h].astype(BF16))
         for h in heads]
    s_new = [s0[h] * hsl(g_last, h) + _dot_tn(stack(u[h], hsl(vr, h)), bkl[h]) for h in heads]
    for h in heads:
        yh_scr[:, h * A_HEAD_DIM:(h + 1) * A_HEAD_DIM] = y[h]
        state_scr[h] = s_new[h]

    y = yh_scr[...]
    inv_n = 1.0 / A_HEAD_DIM
    mu_y = _group_sum(y, sel) * inv_n
    dy = y - mu_y
    var = _group_sum(dy * dy, sel) * inv_n
    yn = dy * lax.rsqrt(var + GN_EPS) * lnw_ref[...] + lnb_ref[...]
    bonus = _group_sum(r * k_mod * rk_ref[...], sel) * vr
    y_ref[...] = ((yn + bonus) * gate_o).astype(y_ref.dtype)

    @pl.when(c == n_chunks - 1)
    def _():
        sout_ref[...] = state_scr[...]


def _rwkv(ua, shift0, s0, p, chunk, n_valid):
    b, t, _ = ua.shape
    n_chunks = t // chunk
    vec = lambda n: pl.BlockSpec((1, n), lambda i, j: (0, 0))
    kern = functools.partial(_rwkv_kernel, chunk=chunk, n_valid=n_valid)
    return pl.pallas_call(
        kern,
        grid=(b, n_chunks),
        in_specs=[pl.BlockSpec((None, chunk, A_COLS), lambda i, j: (i, j, 0)),
                  pl.BlockSpec((None, 1, A_COLS), lambda i, j: (i, 0, 0)),
                  pl.BlockSpec((None, A_HEADS, A_HEAD_DIM, A_HEAD_DIM), lambda i, j: (i, 0, 0, 0)),
                  vec(A_COLS),
                  pl.BlockSpec((LORA_IN, 3 * A_WIDTH), lambda i, j: (0, 0)),
                  vec(A_WIDTH), vec(A_WIDTH), vec(A_WIDTH), vec(A_WIDTH), vec(A_WIDTH), vec(A_WIDTH), vec(A_WIDTH),
                  pl.BlockSpec((LANES, LANES), lambda i, j: (0, 0))],
        out_specs=[pl.BlockSpec((None, chunk, A_WIDTH), lambda i, j: (i, j, 0)),
                   pl.BlockSpec((None, A_HEADS, A_HEAD_DIM, A_HEAD_DIM), lambda i, j: (i, 0, 0, 0))],
        out_shape=[jax.ShapeDtypeStruct((b, t, A_WIDTH), BF16),
                   jax.ShapeDtypeStruct((b, A_HEADS, A_HEAD_DIM, A_HEAD_DIM), F32)],
        scratch_shapes=[pltpu.VMEM((A_HEADS, A_HEAD_DIM, A_HEAD_DIM), F32),
                        pltpu.VMEM((1, A_COLS), F32),
                        pltpu.VMEM((chunk, A_WIDTH), F32)],
        compiler_params=_cparams("arbitrary", "arbitrary"),
        name="rwkv7_chunked",
    )(ua, shift0, s0, p["mu_shift"], p["w_lora"], p["w0"], p["a0"], p["k_k"], p["k_a"], p["r_k"],
      p["ln_x_w"], p["ln_x_b"], p["sel64"])


def _qk_prep_kernel(u_ref, qg_ref, kg_ref, sel_ref, q_ref, kf_ref, kb_ref, vf_ref, vb_ref, *, transposed):
    sel = sel_ref[...]
    wq = B_QK_WIDTH
    uq = u_ref[:, 0:wq]
    uk = u_ref[:, wq:2 * wq]
    uv = u_ref[:, 2 * wq:2 * wq + B_V_WIDTH]
    inv_n = 1.0 / B_QK_DIM
    q = uq * lax.rsqrt(_group_sum(uq * uq, sel) * inv_n + NORM_EPS) * qg_ref[...]
    k = uk * lax.rsqrt(_group_sum(uk * uk, sel) * inv_n + NORM_EPS) * kg_ref[...]
    q = q * (B_QK_DIM ** -0.5)
    kf_ref[...] = k
    kb_ref[...] = k.astype(BF16)
    vf_ref[...] = uv
    if transposed:
        qt = q.T
        ch = lax.broadcasted_iota(jnp.int32, (wq, 1), 0)
        first_map = (ch % (2 * B_QK_DIM)) < B_QK_DIM
        q_ref[0] = jnp.where(first_map, qt, 0.0).astype(BF16)
        q_ref[1] = jnp.where(first_map, 0.0, qt).astype(BF16)
        vb_ref[...] = uv.T.astype(BF16)
    else:
        ch = lax.broadcasted_iota(jnp.int32, (1, wq), 1)
        first_map = (ch % (2 * B_QK_DIM)) < B_QK_DIM
        q_ref[0] = jnp.where(first_map, q, 0.0).astype(BF16)
        q_ref[1] = jnp.where(first_map, 0.0, q).astype(BF16)
        vb_ref[...] = uv.astype(BF16)


def _qk_prep(u_qkv, qg, kg, sel, tm, transposed):
    m = u_qkv.shape[0]
    w = B_QK_WIDTH
    row = lambda n: pl.BlockSpec((tm, n), lambda i: (i, 0))
    if transposed:
        q_spec, q_shape = pl.BlockSpec((2, w, tm), lambda i: (0, 0, i)), (2, w, m)
        v_spec, v_shape = pl.BlockSpec((w, tm), lambda i: (0, i)), (w, m)
    else:
        q_spec, q_shape = pl.BlockSpec((2, tm, w), lambda i: (0, i, 0)), (2, m, w)
        v_spec, v_shape = row(w), (m, w)
    return pl.pallas_call(
        functools.partial(_qk_prep_kernel, transposed=transposed),
        grid=(m // tm,),
        in_specs=[row(3 * w),
                  pl.BlockSpec((1, w), lambda i: (0, 0)), pl.BlockSpec((1, w), lambda i: (0, 0)),
                  pl.BlockSpec((LANES, LANES), lambda i: (0, 0))],
        out_specs=[q_spec, row(w), row(w), row(w), v_spec],
        out_shape=[jax.ShapeDtypeStruct(q_shape, BF16), jax.ShapeDtypeStruct((m, w), F32),
                   jax.ShapeDtypeStruct((m, w), BF16), jax.ShapeDtypeStruct((m, w), F32),
                   jax.ShapeDtypeStruct(v_shape, BF16)],
        compiler_params=_cparams("arbitrary"),
        name="qk_norm",
    )(u_qkv, qg, kg, sel)


def _lambda(lq1_ref, lk1_ref, lq2_ref, lk2_ref, lam_init):
    s1 = jnp.sum(lq1_ref[...] * lk1_ref[...], axis=-1, keepdims=True)
    s2 = jnp.sum(lq2_ref[...] * lk2_ref[...], axis=-1, keepdims=True)
    return jnp.exp(s1) - jnp.exp(s2) + lam_init


def _sub_norm(o, subln, lam_init):
    ms = jnp.mean(o * o, axis=-1, keepdims=True)
    return o * lax.rsqrt(ms + NORM_EPS) * subln * (1.0 - lam_init)


def _flash_kernel(qi_ref, kj_ref, rb_ref, q_ref, k_ref, v_ref, lq1_ref, lk1_ref, lq2_ref, lk2_ref, subln_ref, o_ref,
                  m_scr, l_scr, acc_scr, bias_scr, *, t, lam_init):
    h = pl.program_id(0)
    s = pl.program_id(1)
    i = qi_ref[s]
    j = kj_ref[s]

    @pl.when(s == 0)
    def _():
        kk = lax.broadcasted_iota(jnp.int32, (t, t), 0)
        qq = lax.broadcasted_iota(jnp.int32, (t, t), 1)
        table = lambda b: rb_ref[b, h]
        bias_scr[0] = jnp.where(qq >= kk, _rel_bias_minus_far(qq - kk, table), NEG_INF)
        bias_scr[1] = _rel_bias_minus_far(qq - kk + t, table)

    @pl.when(j == 0)
    def _():
        m_scr[...] = jnp.full(m_scr.shape, NEG_INF, F32)
        l_scr[...] = jnp.zeros(l_scr.shape, F32)
        acc_scr[...] = jnp.zeros(acc_scr.shape, F32)

    def step(bias):
        k = k_ref[...]
        sc = [_dot(k, q_ref[mp]) for mp in range(2)]
        ps = []
        for mp in range(2):
            x = sc[mp] if bias is None else sc[mp] + bias
            m_prev = m_scr[mp]
            m_new = jnp.maximum(m_prev, jnp.max(x, axis=0, keepdims=True))
            alpha = jnp.exp(m_prev - m_new)
            p = jnp.exp(x - m_new)
            l_scr[mp] = alpha * l_scr[mp] + jnp.sum(p, axis=0, keepdims=True)
            acc_scr[mp] = alpha * acc_scr[mp]
            m_scr[mp] = m_new
            ps.append(p.astype(BF16))
        v = v_ref[...]
        for mp in range(2):
            acc_scr[mp] += _dot(v, ps[mp])

    @pl.when(j < i - 1)
    def _():
        step(None)

    @pl.when((j == i - 1) & (i > 0))
    def _():
        step(bias_scr[1])

    @pl.when(j == i)
    def _():
        step(bias_scr[0])
        lam = _lambda(lq1_ref, lk1_ref, lq2_ref, lk2_ref, lam_init)
        o = acc_scr[0] / l_scr[0] - lam * (acc_scr[1] / l_scr[1])
        ms = jnp.mean(o * o, axis=0, keepdims=True)
        o = o * lax.rsqrt(ms + NORM_EPS) * subln_ref[...] * (1.0 - lam_init)
        o_ref[...] = o.T.astype(o_ref.dtype)


def _flash_prompt(qt2, kb, vt, rel_bias, lam_p, subln, lam_init, t):
    n_tok = kb.shape[0]
    nb = n_tok // t
    hd = 2 * B_QK_DIM
    pairs = [(i, j) for i in range(nb) for j in range(i + 1)]
    qi = jnp.asarray([pr[0] for pr in pairs], jnp.int32)
    kj = jnp.asarray([pr[1] for pr in pairs], jnp.int32)
    vec = lambda n: pl.BlockSpec((1, n), lambda h, s, qi, kj: (0, 0))
    kern = functools.partial(_flash_kernel, t=t, lam_init=lam_init)
    grid_spec = pltpu.PrefetchScalarGridSpec(
        num_scalar_prefetch=2,
        grid=(B_HEADS, len(pairs)),
        in_specs=[pl.BlockSpec(memory_space=pltpu.SMEM),
                  pl.BlockSpec((2, hd, t), lambda h, s, qi, kj: (0, h, qi[s])),
                  pl.BlockSpec((t, hd), lambda h, s, qi, kj: (kj[s], h)),
                  pl.BlockSpec((B_V_DIM, t), lambda h, s, qi, kj: (h, kj[s])),
                  vec(B_QK_DIM), vec(B_QK_DIM), vec(B_QK_DIM), vec(B_QK_DIM),
                  pl.BlockSpec((B_V_DIM, 1), lambda h, s, qi, kj: (0, 0))],
        out_specs=pl.BlockSpec((t, B_V_DIM), lambda h, s, qi, kj: (qi[s], h)),
        scratch_shapes=[pltpu.VMEM((2, 1, t), F32), pltpu.VMEM((2, 1, t), F32),
                        pltpu.VMEM((2, B_V_DIM, t), F32), pltpu.VMEM((2, t, t), F32)],
    )
    return pl.pallas_call(
        kern,
        grid_spec=grid_spec,
        out_shape=jax.ShapeDtypeStruct((n_tok, B_V_WIDTH), BF16),
        compiler_params=_cparams("arbitrary", "arbitrary"),
        name="diff_flash_attention",
    )(qi, kj, rel_bias, qt2, kb, vt, *lam_p, subln.reshape(-1, 1))


def _decode_kernel(pt_ref, q_ref, kn_ref, vn_ref, rbrow_ref, lq1_ref, lk1_ref, lq2_ref, lk2_ref, subln_ref,
                   *rest, pages_per_step, page_size, n_new, lam_init):
    P = pages_per_step
    k_refs = rest[0:P]
    v_refs = rest[P:2 * P]
    o_ref = rest[2 * P]
    m_scr, l_scr, acc_scr = rest[2 * P + 1:]
    j = pl.program_id(1)
    last = pl.num_programs(1) - 1
    R = q_ref.shape[0]
    rows_per_map = B_HEADS * n_new
    L = page_size * B_HEADS

    @pl.when(j == 0)
    def _():
        m_scr[...] = jnp.full(m_scr.shape, NEG_INF, F32)
        l_scr[...] = jnp.zeros(l_scr.shape, F32)
        acc_scr[...] = jnp.zeros(acc_scr.shape, F32)

    q = q_ref[...]

    def row_head_query(width):
        r = lax.broadcasted_iota(jnp.int32, (R, width), 0) % rows_per_map
        return r // n_new, r % n_new

    col = lax.broadcasted_iota(jnp.int32, (R, L), 1)
    head, qi = row_head_query(L)
    own_head = (col % B_HEADS) == head

    def page_scores(pi, with_bias):
        k2 = k_refs[pi][...].reshape(L, 2 * B_QK_DIM).astype(BF16)
        s = _dot_nt(q, k2)
        if with_bias:
            dist = qi + page_size - col // B_HEADS
            s = s + _rel_bias_minus_far(dist, lambda b: rbrow_ref[:, b:b + 1])
        return jnp.where(own_head, s, NEG_INF)

    def attend(scores, values):
        m_prev = m_scr[...]
        m_new = m_prev
        for s in scores:
            m_new = jnp.maximum(m_new, jnp.max(s, axis=-1, keepdims=True))
        alpha = jnp.exp(m_prev - m_new)
        ps = [jnp.exp(s - m_new) for s in scores]
        l_new = alpha * l_scr[...]
        acc = alpha * acc_scr[...]
        for p in ps:
            l_new = l_new + jnp.sum(p, axis=-1, keepdims=True)
        for p, v in zip(ps, values):
            acc = acc + _dot(p.astype(BF16), v)
        l_scr[...] = l_new
        acc_scr[...] = acc
        m_scr[...] = m_new

    def page_values():
        return [v_refs[pi][...].reshape(L, B_V_DIM).astype(BF16) for pi in range(P)]

    @pl.when(j != last)
    def _():
        attend([page_scores(pi, False) for pi in range(P)], page_values())

    @pl.when(j == last)
    def _():
        scores = [page_scores(pi, pi == P - 1) for pi in range(P)]
        kn = kn_ref[...]
        s = _dot_nt(q, kn)
        nl = kn.shape[0]
        ncol = lax.broadcasted_iota(jnp.int32, (R, nl), 1)
        nhead, nqi = row_head_query(nl)
        tok = ncol // B_HEADS
        s = s + _rel_bias_minus_far(nqi - tok, lambda b: rbrow_ref[:, b:b + 1])
        s = jnp.where((ncol % B_HEADS == nhead) & (tok <= nqi), s, NEG_INF)
        attend(scores + [s], page_values() + [vn_ref[...]])
        lam = _lambda(lq1_ref, lk1_ref, lq2_ref, lk2_ref, lam_init)
        o = acc_scr[...] / l_scr[...]
        o = o[0:rows_per_map] - lam * o[rows_per_map:2 * rows_per_map]
        o_ref[...] = _sub_norm(o, subln_ref[...], lam_init).astype(o_ref.dtype)


def _decode_attention(page_table, qd, kn, vn, rbrow, lam_p, subln, cache_k, cache_v, layer, lam_init, n_new,
                      pages_per_step):
    nseq, n_pages = page_table.shape
    page_size = cache_k.shape[2]
    P = pages_per_step
    n_steps = n_pages // P
    R = qd.shape[1]
    hd = 2 * B_QK_DIM
    vec = lambda n: pl.BlockSpec((1, n), lambda b, j, pt: (0, 0))

    def page_spec(pi, width):
        return pl.BlockSpec((None, None, page_size, B_HEADS, width),
                            lambda b, j, pt: (layer, pt[b, j * P + pi], 0, 0, 0))

    kern = functools.partial(_decode_kernel, pages_per_step=P, page_size=page_size, n_new=n_new, lam_init=lam_init)
    grid_spec = pltpu.PrefetchScalarGridSpec(
        num_scalar_prefetch=1,
        grid=(nseq, n_steps),
        in_specs=[pl.BlockSpec((None, R, hd), lambda b, j, pt: (b, 0, 0)),
                  pl.BlockSpec((None, kn.shape[1], hd), lambda b, j, pt: (b, 0, 0)),
                  pl.BlockSpec((None, vn.shape[1], B_V_DIM), lambda b, j, pt: (b, 0, 0)),
                  pl.BlockSpec(rbrow.shape, lambda b, j, pt: (0, 0)),
                  vec(B_QK_DIM), vec(B_QK_DIM), vec(B_QK_DIM), vec(B_QK_DIM), vec(B_V_DIM)]
                 + [page_spec(pi, hd) for pi in range(P)] + [page_spec(pi, B_V_DIM) for pi in range(P)],
        out_specs=pl.BlockSpec((None, R // 2, B_V_DIM), lambda b, j, pt: (b, 0, 0)),
        scratch_shapes=[pltpu.VMEM((R, 1), F32), pltpu.VMEM((R, 1), F32), pltpu.VMEM((R, B_V_DIM), F32)],
    )
    return pl.pallas_call(
        kern,
        grid_spec=grid_spec,
        out_shape=jax.ShapeDtypeStruct((nseq, R // 2, B_V_DIM), BF16),
        compiler_params=_cparams("arbitrary", "arbitrary"),
        name="diff_paged_decode",
    )(page_table, qd, kn, vn, rbrow, *lam_p, subln, *([cache_k] * P), *([cache_v] * P))


def _merge_kernel(ya_ref, yb_ref, ga_ref, gb_ref, wa_ref, wb_ref, o_ref):
    ma = _dot(ya_ref[...], wa_ref[...])
    mb = _dot(yb_ref[...], wb_ref[...])
    o_ref[...] = (_sigmoid(ga_ref[...]) * ma + _sigmoid(gb_ref[...]) * mb).astype(o_ref.dtype)


def _merge(ya, yb, gates, wa, wb, tm):
    m = ya.shape[0]
    d = wa.shape[1]
    return pl.pallas_call(
        _merge_kernel,
        grid=(m // tm,),
        in_specs=[pl.BlockSpec((tm, ya.shape[1]), lambda i: (i, 0)),
                  pl.BlockSpec((tm, yb.shape[1]), lambda i: (i, 0)),
                  pl.BlockSpec((tm, d), lambda i: (i, 0)),
                  pl.BlockSpec((tm, d), lambda i: (i, 1)),
                  pl.BlockSpec(wa.shape, lambda i: (0, 0)),
                  pl.BlockSpec(wb.shape, lambda i: (0, 0))],
        out_specs=pl.BlockSpec((tm, d), lambda i: (i, 0)),
        out_shape=jax.ShapeDtypeStruct((m, d), BF16),
        compiler_params=_cparams("arbitrary"),
        name="branch_merge",
    )(ya, yb, gates, gates, wa, wb)


def _out_router_kernel(mg_ref, x_ref, g1_ref, gain_ref, scale_ref, shift_ref, wo_ref, wrh_ref, wrm_ref, br_ref,
                       x1_ref, h2_ref, comb_ref):
    x1 = x_ref[...] + g1_ref[...] * _dot(mg_ref[...], wo_ref[...])
    x1_ref[...] = x1
    h2 = _modnorm(x1, gain_ref[...], scale_ref[...], shift_ref[...])
    h2_ref[...] = h2.astype(BF16)
    hi, mid, _ = _split3(h2)
    logit = _dot(hi, wrh_ref[...]) + _dot(hi, wrm_ref[...]) + _dot(mid, wrh_ref[...]) + br_ref[...]
    lane = lax.broadcasted_iota(jnp.int32, logit.shape, 1)
    big = jnp.int32(2 ** 30)

    def first_max(mask):
        v = jnp.max(jnp.where(mask, logit, NEG_INF), axis=-1, keepdims=True)
        idx = jnp.min(jnp.where(mask & (logit == v), lane, big), axis=-1, keepdims=True)
        return v, idx

    coarse = lane < N_GROUPS
    vc, g_idx = first_max(coarse)
    p_g = 1.0 / jnp.sum(jnp.where(coarse, jnp.exp(logit - vc), 0.0), axis=-1, keepdims=True)
    lo = N_GROUPS + g_idx * EXPERTS_PER_GROUP
    fine = (lane >= lo) & (lane < lo + EXPERTS_PER_GROUP)
    v1, i1 = first_max(fine)
    v2, i2 = first_max(fine & (lane != i1))
    e2 = jnp.exp(v2 - v1)
    p1 = 1.0 / (1.0 + e2)
    comb_ref[...] = jnp.where(lane == i1, p_g * p1, jnp.where(lane == i2, p_g * (e2 * p1), 0.0))


def _out_router(mg, x, g1, gain, scale, shift, wo, wrh, wrm, br, tm):
    m, d = x.shape
    full = lambda a: pl.BlockSpec(a.shape, lambda i: (0, 0))
    row = lambda n: pl.BlockSpec((tm, n), lambda i: (i, 0))
    return pl.pallas_call(
        _out_router_kernel,
        grid=(m // tm,),
        in_specs=[row(d), row(d), _mod_spec(g1, tm), full(gain), _mod_spec(scale, tm), _mod_spec(shift, tm),
                  full(wo), full(wrh), full(wrm), full(br)],
        out_specs=[row(d), row(d), row(ROUTER_LANES)],
        out_shape=[jax.ShapeDtypeStruct((m, d), F32), jax.ShapeDtypeStruct((m, d), BF16),
                   jax.ShapeDtypeStruct((m, ROUTER_LANES), F32)],
        compiler_params=_cparams("arbitrary"),
        name="out_proj_router",
    )(mg, x, g1, gain, scale, shift, wo, wrh, wrm, br)


def _moe_kernel(h_ref, comb_ref, x1_ref, g2_ref, wg_ref, wu_ref, wd_ref, o_ref, acc_scr):
    e = pl.program_id(1)

    @pl.when(e == 0)
    def _():
        acc_scr[...] = jnp.zeros(acc_scr.shape, F32)

    h = h_ref[...]
    hg = _dot(h, wg_ref[...])
    hu = _dot(h, wu_ref[...])
    lane = lax.broadcasted_iota(jnp.int32, comb_ref.shape, 1)
    w = jnp.sum(jnp.where(lane == e + N_GROUPS, comb_ref[...], 0.0), axis=-1, keepdims=True)
    act = _silu(hg) * hu * w
    acc_scr[...] += _dot(act.astype(BF16), wd_ref[...])

    @pl.when(e == pl.num_programs(1) - 1)
    def _():
        o_ref[...] = x1_ref[...] + g2_ref[...] * acc_scr[...]


def _moe(h2, comb, x1, g2, wg, wu, wd, tm):
    m, d = x1.shape
    ne, _, f = wg.shape
    row = lambda n: pl.BlockSpec((tm, n), lambda i, e: (i, 0))
    return pl.pallas_call(
        _moe_kernel,
        grid=(m // tm, ne),
        in_specs=[row(d), row(ROUTER_LANES), row(d), _mod_spec(g2, tm),
                  pl.BlockSpec((None, d, f), lambda i, e: (e, 0, 0)),
                  pl.BlockSpec((None, d, f), lambda i, e: (e, 0, 0)),
                  pl.BlockSpec((None, f, d), lambda i, e: (e, 0, 0))],
        out_specs=row(d),
        out_shape=jax.ShapeDtypeStruct((m, d), F32),
        scratch_shapes=[pltpu.VMEM((tm, d), F32)],
        compiler_params=_cparams("arbitrary", "arbitrary"),
        name="moe_experts",
    )(h2, comb, x1, g2, wg, wu, wd)


def _pick(m, prefs):
    for t in prefs:
        if m % t == 0:
            return t
    return m


def _prep_layer_params(l, w_in, mu_shift, w0, w_decay_up, a0, w_iclr_up, w_gate_up, k_k, k_a, r_k, ln_x_w, ln_x_b,
                       q_norm, k_norm, lambda_q1, lambda_k1, lambda_q2, lambda_k2, subln, w_branch_a, w_branch_b,
                       w_out, w_router_coarse, b_router_coarse, w_router_fine, b_router_fine,
                       w_exp_gate, w_exp_up, w_exp_down, norm1, norm2):
    d = w_in.shape[1]
    wi = w_in[l]
    c_qkv = A_COLS + 2 * B_QK_WIDTH + B_V_WIDTH
    w_lora = jnp.zeros((LORA_IN, 3 * A_WIDTH), F32)
    w_lora = w_lora.at[0:DECAY_LORA, 0:A_WIDTH].set(w_decay_up[l])
    w_lora = w_lora.at[DECAY_LORA:DECAY_LORA + ICLR_LORA, A_WIDTH:2 * A_WIDTH].set(w_iclr_up[l])
    w_lora = w_lora.at[DECAY_LORA + ICLR_LORA:, 2 * A_WIDTH:].set(w_gate_up[l])
    lane = jnp.arange(LANES)
    sel64 = (lane[:, None] // A_HEAD_DIM == lane[None, :] // A_HEAD_DIM).astype(BF16)
    w_r = jnp.zeros((d, ROUTER_LANES), F32)
    w_r = w_r.at[:, 0:N_GROUPS].set(w_router_coarse[l])
    w_r = w_r.at[:, N_GROUPS:N_GROUPS + N_EXPERTS].set(
        jnp.transpose(w_router_fine[l], (1, 0, 2)).reshape(d, N_EXPERTS))
    w_rh = w_r.astype(BF16)
    w_rm = (w_r - w_rh.astype(F32)).astype(BF16)
    b_r = jnp.zeros((1, ROUTER_LANES), F32)
    b_r = b_r.at[0, 0:N_GROUPS].set(b_router_coarse[l])
    b_r = b_r.at[0, N_GROUPS:N_GROUPS + N_EXPERTS].set(b_router_fine[l].reshape(-1))
    row = lambda a: a.reshape(1, -1)
    return dict(
        w_in_a=wi[:, 0:A_COLS].astype(BF16), w_in_qkv=wi[:, A_COLS:c_qkv].astype(BF16),
        w_in_g=wi[:, c_qkv:].astype(BF16),
        mu_shift=row(mu_shift[l]), w_lora=w_lora.astype(BF16), w0=row(w0[l]), a0=row(a0[l]), k_k=row(k_k[l]),
        k_a=row(k_a[l]), r_k=row(r_k[l]), ln_x_w=row(ln_x_w[l]), ln_x_b=row(ln_x_b[l]), sel64=sel64,
        q_gain=row(jnp.tile(q_norm[l].reshape(-1), B_HEADS)), k_gain=row(jnp.tile(k_norm[l].reshape(-1), B_HEADS)),
        lam=(row(lambda_q1[l]), row(lambda_k1[l]), row(lambda_q2[l]), row(lambda_k2[l])), subln=row(subln[l]),
        w_a=w_branch_a[l].astype(BF16), w_b=w_branch_b[l].astype(BF16), w_out=w_out[l].astype(BF16),
        w_rh=w_rh, w_rm=w_rm, b_r=b_r,
        w_eg=w_exp_gate[l].astype(BF16), w_eu=w_exp_up[l].astype(BF16), w_ed=w_exp_down[l].astype(BF16),
        norm1=row(norm1[l]), norm2=row(norm2[l]))


def _trunk_rows(x2, mods, p, tm):
    sh1, sc1 = mods[0], mods[1]
    tm_a = _pick(x2.shape[0], (tm,))
    ua = _norm_proj(x2, p["norm1"], sc1, sh1, p["w_in_a"], tm_a, A_COLS // 2)
    uqkv = _norm_proj(x2, p["norm1"], sc1, sh1, p["w_in_qkv"], tm_a, 1024)
    ug = _norm_proj(x2, p["norm1"], sc1, sh1, p["w_in_g"], tm_a, 1024)
    return ua, uqkv, ug


def _tail_rows(x2, ya, yb, ug, mods, p, tm):
    _, _, g1, sh2, sc2, g2 = mods
    m = x2.shape[0]
    mg = _merge(ya, yb, ug, p["w_a"], p["w_b"], _pick(m, (tm,)))
    x1, h2, comb = _out_router(mg, x2, g1, p["norm2"], sc2, sh2, p["w_out"], p["w_rh"], p["w_rm"], p["b_r"],
                               _pick(m, (256, 128)))
    return _moe(h2, comb, x1, g2, p["w_eg"], p["w_eu"], p["w_ed"], _pick(m, (tm,)))


def kernel(x_prompt, x_sample, c_prompt, c_sample, cache_k, cache_v, state_wkv, state_shift, page_table, rel_bias, w_ada, b_ada, norm1, norm2, w_in, mu_shift, w0, w_decay_up, a0, w_iclr_up, w_gate_up, k_k, k_a, r_k, ln_x_w, ln_x_b, q_norm, k_norm, lambda_q1, lambda_k1, lambda_q2, lambda_k2, subln, w_branch_a, w_branch_b, w_out, w_router_coarse, b_router_coarse, w_router_fine, b_router_fine, w_exp_gate, w_exp_up, w_exp_down):
    depth = w_in.shape[0]
    bp, tp, d = x_prompt.shape
    bs, ts, _ = x_sample.shape
    n_new_pad = 8
    kv_new_rows = LANES // B_HEADS
    yp = x_prompt.reshape(bp * tp, d)
    ys = x_sample.reshape(bs * ts, d)
    c_all = jnp.concatenate([c_prompt, c_sample], axis=0)
    c_rows = -(-c_all.shape[0] // 8) * 8
    c_all = jnp.pad(c_all, ((0, c_rows - c_all.shape[0]), (0, 0)))
    outs = [[] for _ in range(8)]
    for l in range(depth):
        lam_init = 0.8 - 0.6 * math.exp(-0.3 * l)
        p = _prep_layer_params(l, w_in, mu_shift, w0, w_decay_up, a0, w_iclr_up, w_gate_up, k_k, k_a, r_k, ln_x_w,
                               ln_x_b, q_norm, k_norm, lambda_q1, lambda_k1, lambda_q2, lambda_k2, subln,
                               w_branch_a, w_branch_b, w_out, w_router_coarse, b_router_coarse, w_router_fine,
                               b_router_fine, w_exp_gate, w_exp_up, w_exp_down, norm1, norm2)
        ada = _ada(c_all, w_ada[l], b_ada[l])
        mods_p = [jnp.repeat(a, tp, axis=0) if bp > 1 else a for a in jnp.split(ada[0:bp], 6, axis=-1)]
        ua, uqkv, ug = _trunk_rows(yp, mods_p, p, 512)
        ua3 = ua.reshape(bp, tp, A_COLS)
        ya, wkv_p = _rwkv(ua3, jnp.zeros((bp, 1, A_COLS), F32),
                          jnp.zeros((bp, A_HEADS, A_HEAD_DIM, A_HEAD_DIM), F32), p, 64, None)
        qt2, kf, kb, vf, vt = _qk_prep(uqkv, p["q_gain"], p["k_gain"], p["sel64"], 512, True)
        t_blk = _pick(tp, (512, 256, 128))
        yb = jnp.concatenate(
            [_flash_prompt(qt2[:, :, b * tp:(b + 1) * tp], kb[b * tp:(b + 1) * tp], vt[:, b * tp:(b + 1) * tp],
                           rel_bias, p["lam"], p["subln"], lam_init, t_blk) for b in range(bp)], axis=0)
        yp = _tail_rows(yp, ya.reshape(bp * tp, A_WIDTH), yb, ug, mods_p, p, 512)
        outs[0].append(kf.reshape(bp, tp, B_HEADS, 2 * B_QK_DIM))
        outs[1].append(vf.reshape(bp, tp, B_HEADS, B_V_DIM))
        outs[2].append(wkv_p)
        outs[3].append(ua3[:, -1])
        mods_s = [jnp.repeat(a, ts, axis=0) for a in jnp.split(ada[bp:bp + bs], 6, axis=-1)]
        ua, uqkv, ug = _trunk_rows(ys, mods_s, p, 128)
        ua3 = ua.reshape(bs, ts, A_COLS)
        ua_pad = jnp.pad(ua3, ((0, 0), (0, n_new_pad - ts), (0, 0)))
        ya, wkv_s = _rwkv(ua_pad, state_shift[l][:, None, :], state_wkv[l], p, n_new_pad, ts)
        ya = ya[:, 0:ts].reshape(bs * ts, A_WIDTH)
        q2, kf, kb, vf, vb = _qk_prep(uqkv, p["q_gain"], p["k_gain"], p["sel64"], _pick(bs * ts, (128,)), False)
        qd = q2.reshape(2, bs, ts, B_HEADS, 2 * B_QK_DIM).transpose(1, 0, 3, 2, 4).reshape(
            bs, 2 * B_HEADS * ts, 2 * B_QK_DIM)
        kn = jnp.pad(kb.reshape(bs, ts * B_HEADS, 2 * B_QK_DIM), ((0, 0), (0, (kv_new_rows - ts) * B_HEADS), (0, 0)))
        vn = jnp.pad(vb.reshape(bs, ts * B_HEADS, B_V_DIM), ((0, 0), (0, (kv_new_rows - ts) * B_HEADS), (0, 0)))
        rbrow = jnp.tile(jnp.repeat(rel_bias.T, ts, axis=0), (2, 1))
        od = _decode_attention(page_table, qd, kn, vn, rbrow, p["lam"], p["subln"], cache_k, cache_v, l, lam_init,
                               ts, _pick(page_table.shape[1], (8, 4, 2, 1)))
        yb = od.reshape(bs, B_HEADS, ts, B_V_DIM).transpose(0, 2, 1, 3).reshape(bs * ts, B_V_WIDTH)
        ys = _tail_rows(ys, ya, yb, ug, mods_s, p, 128)
        outs[4].append(kf.reshape(bs, ts, B_HEADS, 2 * B_QK_DIM))
        outs[5].append(vf.reshape(bs, ts, B_HEADS, B_V_DIM))
        outs[6].append(wkv_s)
        outs[7].append(ua3[:, -1])
    st = [jnp.stack(o) for o in outs]
    return (yp.reshape(bp, tp, d), ys.reshape(bs, ts, d), st[0], st[1], st[2], st[3], st[4], st[5], st[6], st[7])
```

```python
import functools
import math

import jax
import jax.numpy as jnp
from jax import lax
from jax.experimental import pallas as pl
from jax.experimental.pallas import tpu as pltpu

F32 = jnp.float32
BF16 = jnp.bfloat16

A_HEADS = 16
A_HEAD_DIM = 64
A_WIDTH = A_HEADS * A_HEAD_DIM
DECAY_LORA = 64
ICLR_LORA = 64
GATE_LORA = 128
LORA_IN = DECAY_LORA + ICLR_LORA + GATE_LORA
A_COLS = 3 * A_WIDTH + LORA_IN
GN_EPS = 64e-5
B_HEADS = 8
B_QK_DIM = 64
B_V_DIM = 2 * B_QK_DIM
B_QK_WIDTH = B_HEADS * 2 * B_QK_DIM
B_V_WIDTH = B_HEADS * B_V_DIM
NUM_BUCKETS = 32
MAX_DISTANCE = 128
N_GROUPS = 4
EXPERTS_PER_GROUP = 8
N_EXPERTS = N_GROUPS * EXPERTS_PER_GROUP
NORM_EPS = 1e-6
NEG_INF = -1e30

LANES = 128
ROUTER_LANES = LANES
VMEM_LIMIT = 56 * 1024 * 1024


def _cparams(*sem):
    return pltpu.CompilerParams(dimension_semantics=sem, vmem_limit_bytes=VMEM_LIMIT)


def _dot(a, b):
    return jnp.dot(a, b, preferred_element_type=F32)


def _dot_nt(a, b):
    return lax.dot_general(a, b, (((1,), (1,)), ((), ())), preferred_element_type=F32)


def _dot_tn(a, b):
    return lax.dot_general(a, b, (((0,), (0,)), ((), ())), preferred_element_type=F32)


def _split3(x):
    hi = x.astype(BF16)
    r1 = x - hi.astype(F32)
    mid = r1.astype(BF16)
    lo = (r1 - mid.astype(F32)).astype(BF16)
    return hi, mid, lo


def _dot_sel_rhs(x, sel):
    hi, mid, lo = _split3(x)
    return _dot(hi, sel) + _dot(mid, sel) + _dot(lo, sel)


def _dot_sel_lhs(sel, x):
    hi, mid, lo = _split3(x)
    return _dot(sel, hi) + _dot(sel, mid) + _dot(sel, lo)


def _group_sum(x, sel):
    parts = [_dot_sel_rhs(x[:, t * LANES:(t + 1) * LANES], sel) for t in range(x.shape[1] // LANES)]
    return parts[0] if len(parts) == 1 else jnp.concatenate(parts, axis=1)


def _sigmoid(x):
    return 1.0 / (1.0 + jnp.exp(-x))


def _silu(x):
    return x * _sigmoid(x)


def _rel_bias_minus_far(dist, table_fn):
    n = jnp.maximum(dist, 0)
    max_exact = NUM_BUCKETS // 2
    nf = jnp.maximum(n, 1).astype(F32)
    large = max_exact + (jnp.log(nf / max_exact) / math.log(MAX_DISTANCE / max_exact)
                         * (NUM_BUCKETS - max_exact)).astype(jnp.int32)
    large = jnp.minimum(large, NUM_BUCKETS - 1)
    bucket = jnp.where(n < max_exact, n, large)
    far = table_fn(NUM_BUCKETS - 1)
    out = jnp.zeros(dist.shape, F32)
    for b in range(NUM_BUCKETS - 1):
        out = jnp.where(bucket == b, table_fn(b) - far, out)
    return out


def _ada_kernel(c_ref, w_ref, b_ref, o_ref):
    s = _silu(c_ref[...])
    o_ref[...] = _dot(s.astype(BF16), w_ref[...].astype(BF16)) + b_ref[...]


def _ada(c, w, b):
    m, d = c.shape
    n = w.shape[1]
    tn = 1024
    return pl.pallas_call(
        _ada_kernel,
        grid=(n // tn,),
        in_specs=[pl.BlockSpec((m, d), lambda j: (0, 0)),
                  pl.BlockSpec((d, tn), lambda j: (0, j)),
                  pl.BlockSpec((1, tn), lambda j: (0, j))],
        out_specs=pl.BlockSpec((m, tn), lambda j: (0, j)),
        out_shape=jax.ShapeDtypeStruct((m, n), F32),
        compiler_params=_cparams("arbitrary"),
        name="ada_modulation",
    )(c, w, b.reshape(1, n))


def _modnorm(x, gain, scale, shift):
    ms = jnp.mean(x * x, axis=-1, keepdims=True)
    h = x * lax.rsqrt(ms + NORM_EPS) * gain
    return h * (1.0 + scale) + shift


def _norm_proj_kernel(x_ref, gain_ref, scale_ref, shift_ref, w_ref, o_ref, h_scr):
    @pl.when(pl.program_id(1) == 0)
    def _():
        h_scr[...] = _modnorm(x_ref[...], gain_ref[...], scale_ref[...], shift_ref[...]).astype(BF16)

    o_ref[...] = _dot(h_scr[...], w_ref[...])


def _mod_spec(mod, tm):
    if mod.shape[0] == 1:
        return pl.BlockSpec((1, mod.shape[1]), lambda i, *_: (0, 0))
    return pl.BlockSpec((tm, mod.shape[1]), lambda i, *_: (i, 0))


def _norm_proj(x, gain, scale, shift, w, tm, tn):
    m, d = x.shape
    n = w.shape[1]
    return pl.pallas_call(
        _norm_proj_kernel,
        grid=(m // tm, n // tn),
        in_specs=[pl.BlockSpec((tm, d), lambda i, j: (i, 0)),
                  pl.BlockSpec((1, d), lambda i, j: (0, 0)),
                  _mod_spec(scale, tm), _mod_spec(shift, tm),
                  pl.BlockSpec((d, tn), lambda i, j: (0, j))],
        out_specs=pl.BlockSpec((tm, tn), lambda i, j: (i, j)),
        out_shape=jax.ShapeDtypeStruct((m, n), F32),
        scratch_shapes=[pltpu.VMEM((tm, d), BF16)],
        compiler_params=_cparams("arbitrary", "arbitrary"),
        name="norm_in_proj",
    )(x, gain, scale, shift, w)


def _rwkv_kernel(ua_ref, shift0_ref, s0_ref, mu_ref, wl_ref, w0_ref, a0_ref, kk_ref, ka_ref, rk_ref,
                 lnw_ref, lnb_ref, sel_ref, y_ref, sout_ref, state_scr, prev_scr, yh_scr, *, chunk, n_valid):
    c = pl.program_id(1)
    n_chunks = pl.num_programs(1)
    C = chunk
    W = A_WIDTH

    @pl.when(c == 0)
    def _():
        state_scr[...] = s0_ref[...]
        prev_scr[...] = shift0_ref[...]

    ua = ua_ref[...]
    row = lax.broadcasted_iota(jnp.int32, (C, 1), 0)
    prev = jnp.where(row == 0, prev_scr[...], pltpu.roll(ua, 1, axis=0))
    prev_scr[...] = ua[C - 1:C, :]
    us = ua + (prev - ua) * mu_ref[...]
    r = us[:, 0:W]
    kr = us[:, W:2 * W]
    vr = us[:, 2 * W:3 * W]
    lin = us[:, 3 * W:3 * W + LORA_IN]
    lane = lax.broadcasted_iota(jnp.int32, (1, LORA_IN), 1)
    act = jnp.where(lane < DECAY_LORA, jnp.tanh(lin),
                    jnp.where(lane < DECAY_LORA + ICLR_LORA, lin, _sigmoid(lin)))
    lora = _dot(act.astype(BF16), wl_ref[...])
    z = -(w0_ref[...] + lora[:, 0:W])
    softplus = jnp.maximum(z, 0.0) + jnp.log(1.0 + jnp.exp(-jnp.abs(z)))
    ld = -jnp.exp(-softplus - 0.5)
    iclr = _sigmoid(a0_ref[...] + lora[:, W:2 * W])
    gate_o = lora[:, 2 * W:3 * W]

    sel = sel_ref[...]
    kk = kr * kk_ref[...]
    kk = kk / jnp.maximum(jnp.sqrt(_group_sum(kk * kk, sel)), 1e-12)
    k_mod = kr * (1.0 + (iclr - 1.0) * ka_ref[...])
    if n_valid is not None:
        valid = (c * C + row) < n_valid
        ld = jnp.where(valid, ld, 0.0)
        kk = jnp.where(valid, kk, 0.0)
        k_mod = jnp.where(valid, k_mod, 0.0)
        vr = jnp.where(valid, vr, 0.0)

    ri = lax.broadcasted_iota(jnp.int32, (C, C), 0)
    ci = lax.broadcasted_iota(jnp.int32, (C, C), 1)
    cum = _dot_sel_lhs((ci <= ri).astype(BF16), ld)
    g_inv = jnp.exp(-cum)
    g_last = jnp.exp(cum[C - 1:C, :])
    at = -kk * jnp.exp(cum - ld)
    bt = kk * iclr * g_inv
    kt = k_mod * g_inv
    bl = bt * g_last
    kl = kt * g_last
    rt = r * jnp.exp(cum)
    eye = (ci == ri).astype(F32)
    n_double = max(int(math.ceil(math.log2(C))) - 1, 0)
    r2 = lax.broadcasted_iota(jnp.int32, (2 * C, 2 * C), 0)
    c2 = lax.broadcasted_iota(jnp.int32, (2 * C, 2 * C), 1)
    mask2 = ((r2 < C) & ((c2 % C) < r2)) | ((r2 >= C) & ((c2 % C) <= (r2 - C)))
    zeros_c = jnp.zeros((C, A_HEAD_DIM), F32)
    heads = range(A_HEADS)

    def hsl(x, h):
        return x[:, h * A_HEAD_DIM:(h + 1) * A_HEAD_DIM]

    def stack(top, bottom):
        return jnp.concatenate([top, bottom], axis=0).astype(BF16)

    atb = [hsl(at, h).astype(BF16) for h in heads]
    ar = [stack(hsl(at, h), hsl(rt, h)) for h in heads]
    bk = [stack(hsl(bt, h), hsl(kt, h)) for h in heads]
    zv = [stack(zeros_c, hsl(vr, h)) for h in heads]
    bkl = [stack(hsl(bl, h), hsl(kl, h)) for h in heads]
    m2 = [jnp.where(mask2, _dot_nt(ar[h], bk[h]), 0.0) for h in heads]
    m2b = [m.astype(BF16) for m in m2]
    bvk = [_dot(m2b[h], zv[h]) for h in heads]
    p = [m[0:C, 0:C] for m in m2]
    t_inv = [eye + a for a in p]
    for _ in range(n_double):
        pb = [x.astype(BF16) for x in p]
        p = [_dot(x, x) for x in pb]
        t_inv = [t + _dot(t.astype(BF16), x.astype(BF16)) for t, x in zip(t_inv, p)]
    tb = [t.astype(BF16) for t in t_inv]
    w = [_dot(tb[h], atb[h]) for h in heads]
    z = [_dot(tb[h], bvk[h][0:C].astype(BF16)) for h in heads]
    s0 = [state_scr[h] for h in heads]
    ws = [_dot_nt(stack(w[h], hsl(rt, h)), s0[h].astype(BF16)) for h in heads]
    u = [ws[h][0:C] + z[h] for h in heads]
    y = [ws[h][C:2 * C] + bvk[h][C:2 * C] + _dot(m2[h][C:2 * C, 0:C].astype(BF16), u[h].astype(BF16))
         for h in heads]
    s_new = [s0[h] * hsl(g_last, h) + _dot_tn(stack(u[h], hsl(vr, h)), bkl[h]) for h in heads]
    for h in heads:
        yh_scr[:, h * A_HEAD_DIM:(h + 1) * A_HEAD_DIM] = y[h]
        state_scr[h] = s_new[h]

    y = yh_scr[...]
    inv_n = 1.0 / A_HEAD_DIM
    mu_y = _group_sum(y, sel) * inv_n
    dy = y - mu_y
    var = _group_sum(dy * dy, sel) * inv_n
    yn = dy * lax.rsqrt(var + GN_EPS) * lnw_ref[...] + lnb_ref[...]
    bonus = _group_sum(r * k_mod * rk_ref[...], sel) * vr
    y_ref[...] = ((yn + bonus) * gate_o).astype(y_ref.dtype)

    @pl.when(c == n_chunks - 1)
    def _():
        sout_ref[...] = state_scr[...]


def _rwkv(ua, shift0, s0, p, chunk, n_valid):
    b, t, _ = ua.shape
    n_chunks = t // chunk
    vec = lambda n: pl.BlockSpec((1, n), lambda i, j: (0, 0))
    kern = functools.partial(_rwkv_kernel, chunk=chunk, n_valid=n_valid)
    return pl.pallas_call(
        kern,
        grid=(b, n_chunks),
        in_specs=[pl.BlockSpec((None, chunk, A_COLS), lambda i, j: (i, j, 0)),
                  pl.BlockSpec((None, 1, A_COLS), lambda i, j: (i, 0, 0)),
                  pl.BlockSpec((None, A_HEADS, A_HEAD_DIM, A_HEAD_DIM), lambda i, j: (i, 0, 0, 0)),
                  vec(A_COLS),
                  pl.BlockSpec((LORA_IN, 3 * A_WIDTH), lambda i, j: (0, 0)),
                  vec(A_WIDTH), vec(A_WIDTH), vec(A_WIDTH), vec(A_WIDTH), vec(A_WIDTH), vec(A_WIDTH), vec(A_WIDTH),
                  pl.BlockSpec((LANES, LANES), lambda i, j: (0, 0))],
        out_specs=[pl.BlockSpec((None, chunk, A_WIDTH), lambda i, j: (i, j, 0)),
                   pl.BlockSpec((None, A_HEADS, A_HEAD_DIM, A_HEAD_DIM), lambda i, j: (i, 0, 0, 0))],
        out_shape=[jax.ShapeDtypeStruct((b, t, A_WIDTH), BF16),
                   jax.ShapeDtypeStruct((b, A_HEADS, A_HEAD_DIM, A_HEAD_DIM), F32)],
        scratch_shapes=[pltpu.VMEM((A_HEADS, A_HEAD_DIM, A_HEAD_DIM), F32),
                        pltpu.VMEM((1, A_COLS), F32),
                        pltpu.VMEM((chunk, A_WIDTH), F32)],
        compiler_params=_cparams("arbitrary", "arbitrary"),
        name="rwkv7_chunked",
    )(ua, shift0, s0, p["mu_shift"], p["w_lora"], p["w0"], p["a0"], p["k_k"], p["k_a"], p["r_k"],
      p["ln_x_w"], p["ln_x_b"], p["sel64"])


def _qk_prep_kernel(u_ref, qg_ref, kg_ref, sel_ref, q_ref, kf_ref, kb_ref, vf_ref, vb_ref, *, transposed):
    sel = sel_ref[...]
    wq = B_QK_WIDTH
    uq = u_ref[:, 0:wq]
    uk = u_ref[:, wq:2 * wq]
    uv = u_ref[:, 2 * wq:2 * wq + B_V_WIDTH]
    inv_n = 1.0 / B_QK_DIM
    q = uq * lax.rsqrt(_group_sum(uq * uq, sel) * inv_n + NORM_EPS) * qg_ref[...]
    k = uk * lax.rsqrt(_group_sum(uk * uk, sel) * inv_n + NORM_EPS) * kg_ref[...]
    q = q * (B_QK_DIM ** -0.5)
    kf_ref[...] = k
    kb_ref[...] = k.astype(BF16)
    vf_ref[...] = uv
    if transposed:
        qt = q.T
        ch = lax.broadcasted_iota(jnp.int32, (wq, 1), 0)
        first_map = (ch % (2 * B_QK_DIM)) < B_QK_DIM
        q_ref[0] = jnp.where(first_map, qt, 0.0).astype(BF16)
        q_ref[1] = jnp.where(first_map, 0.0, qt).astype(BF16)
        vb_ref[...] = uv.T.astype(BF16)
    else:
        ch = lax.broadcasted_iota(jnp.int32, (1, wq), 1)
        first_map = (ch % (2 * B_QK_DIM)) < B_QK_DIM
        q_ref[0] = jnp.where(first_map, q, 0.0).astype(BF16)
        q_ref[1] = jnp.where(first_map, 0.0, q).astype(BF16)
        vb_ref[...] = uv.astype(BF16)


def _qk_prep(u_qkv, qg, kg, sel, tm, transposed):
    m = u_qkv.shape[0]
    w = B_QK_WIDTH
    row = lambda n: pl.BlockSpec((tm, n), lambda i: (i, 0))
    if transposed:
        q_spec, q_shape = pl.BlockSpec((2, w, tm), lambda i: (0, 0, i)), (2, w, m)
        v_spec, v_shape = pl.BlockSpec((w, tm), lambda i: (0, i)), (w, m)
    else:
        q_spec, q_shape = pl.BlockSpec((2, tm, w), lambda i: (0, i, 0)), (2, m, w)
        v_spec, v_shape = row(w), (m, w)
    return pl.pallas_call(
        functools.partial(_qk_prep_kernel, transposed=transposed),
        grid=(m // tm,),
        in_specs=[row(3 * w),
                  pl.BlockSpec((1, w), lambda i: (0, 0)), pl.BlockSpec((1, w), lambda i: (0, 0)),
                  pl.BlockSpec((LANES, LANES), lambda i: (0, 0))],
        out_specs=[q_spec, row(w), row(w), row(w), v_spec],
        out_shape=[jax.ShapeDtypeStruct(q_shape, BF16), jax.ShapeDtypeStruct((m, w), F32),
                   jax.ShapeDtypeStruct((m, w), BF16), jax.ShapeDtypeStruct((m, w), F32),
                   jax.ShapeDtypeStruct(v_shape, BF16)],
        compiler_params=_cparams("arbitrary"),
        name="qk_norm",
    )(u_qkv, qg, kg, sel)


def _lambda(lq1_ref, lk1_ref, lq2_ref, lk2_ref, lam_init):
    s1 = jnp.sum(lq1_ref[...] * lk1_ref[...], axis=-1, keepdims=True)
    s2 = jnp.sum(lq2_ref[...] * lk2_ref[...], axis=-1, keepdims=True)
    return jnp.exp(s1) - jnp.exp(s2) + lam_init


def _sub_norm(o, subln, lam_init):
    ms = jnp.mean(o * o, axis=-1, keepdims=True)
    return o * lax.rsqrt(ms + NORM_EPS) * subln * (1.0 - lam_init)


def _flash_kernel(qi_ref, kj_ref, rb_ref, q_ref, k_ref, v_ref, lq1_ref, lk1_ref, lq2_ref, lk2_ref, subln_ref, o_ref,
                  m_scr, l_scr, acc_scr, bias_scr, *, t, lam_init):
    h = pl.program_id(0)
    s = pl.program_id(1)
    i = qi_ref[s]
    j = kj_ref[s]

    @pl.when(s == 0)
    def _():
        kk = lax.broadcasted_iota(jnp.int32, (t, t), 0)
        qq = lax.broadcasted_iota(jnp.int32, (t, t), 1)
        table = lambda b: rb_ref[b, h]
        bias_scr[0] = jnp.where(qq >= kk, _rel_bias_minus_far(qq - kk, table), NEG_INF)
        bias_scr[1] = _rel_bias_minus_far(qq - kk + t, table)

    @pl.when(j == 0)
    def _():
        m_scr[...] = jnp.full(m_scr.shape, NEG_INF, F32)
        l_scr[...] = jnp.zeros(l_scr.shape, F32)
        acc_scr[...] = jnp.zeros(acc_scr.shape, F32)

    def step(bias):
        k = k_ref[...]
        sc = [_dot(k, q_ref[mp]) for mp in range(2)]
        ps = []
        for mp in range(2):
            x = sc[mp] if bias is None else sc[mp] + bias
            m_prev = m_scr[mp]
            m_new = jnp.maximum(m_prev, jnp.max(x, axis=0, keepdims=True))
            alpha = jnp.exp(m_prev - m_new)
            p = jnp.exp(x - m_new)
            l_scr[mp] = alpha * l_scr[mp] + jnp.sum(p, axis=0, keepdims=True)
            acc_scr[mp] = alpha * acc_scr[mp]
            m_scr[mp] = m_new
            ps.append(p.astype(BF16))
        v = v_ref[...]
        for mp in range(2):
            acc_scr[mp] += _dot(v, ps[mp])

    @pl.when(j < i - 1)
    def _():
        step(None)

    @pl.when((j == i - 1) & (i > 0))
    def _():
        step(bias_scr[1])

    @pl.when(j == i)
    def _():
        step(bias_scr[0])
        lam = _lambda(lq1_ref, lk1_ref, lq2_ref, lk2_ref, lam_init)
        o = acc_scr[0] / l_scr[0] - lam * (acc_scr[1] / l_scr[1])
        ms = jnp.mean(o * o, axis=0, keepdims=True)
        o = o * lax.rsqrt(ms + NORM_EPS) * subln_ref[...] * (1.0 - lam_init)
        o_ref[...] = o.T.astype(o_ref.dtype)


def _flash_prompt(qt2, kb, vt, rel_bias, lam_p, subln, lam_init, t):
    n_tok = kb.shape[0]
    nb = n_tok // t
    hd = 2 * B_QK_DIM
    pairs = [(i, j) for i in range(nb) for j in range(i + 1)]
    qi = jnp.asarray([pr[0] for pr in pairs], jnp.int32)
    kj = jnp.asarray([pr[1] for pr in pairs], jnp.int32)
    vec = lambda n: pl.BlockSpec((1, n), lambda h, s, qi, kj: (0, 0))
    kern = functools.partial(_flash_kernel, t=t, lam_init=lam_init)
    grid_spec = pltpu.PrefetchScalarGridSpec(
        num_scalar_prefetch=2,
        grid=(B_HEADS, len(pairs)),
        in_specs=[pl.BlockSpec(memory_space=pltpu.SMEM),
                  pl.BlockSpec((2, hd, t), lambda h, s, qi, kj: (0, h, qi[s])),
                  pl.BlockSpec((t, hd), lambda h, s, qi, kj: (kj[s], h)),
                  pl.BlockSpec((B_V_DIM, t), lambda h, s, qi, kj: (h, kj[s])),
                  vec(B_QK_DIM), vec(B_QK_DIM), vec(B_QK_DIM), vec(B_QK_DIM),
                  pl.BlockSpec((B_V_DIM, 1), lambda h, s, qi, kj: (0, 0))],
        out_specs=pl.BlockSpec((t, B_V_DIM), lambda h, s, qi, kj: (qi[s], h)),
        scratch_shapes=[pltpu.VMEM((2, 1, t), F32), pltpu.VMEM((2, 1, t), F32),
                        pltpu.VMEM((2, B_V_DIM, t), F32), pltpu.VMEM((2, t, t), F32)],
    )
    return pl.pallas_call(
        kern,
        grid_spec=grid_spec,
        out_shape=jax.ShapeDtypeStruct((n_tok, B_V_WIDTH), BF16),
        compiler_params=_cparams("arbitrary", "arbitrary"),
        name="diff_flash_attention",
    )(qi, kj, rel_bias, qt2, kb, vt, *lam_p, subln.reshape(-1, 1))


def _decode_kernel(pt_ref, q_ref, kn_ref, vn_ref, rbrow_ref, lq1_ref, lk1_ref, lq2_ref, lk2_ref, subln_ref,
                   *rest, pages_per_step, page_size, n_new, lam_init):
    P = pages_per_step
    k_refs = rest[0:P]
    v_refs = rest[P:2 * P]
    o_ref = rest[2 * P]
    m_scr, l_scr, acc_scr = rest[2 * P + 1:]
    j = pl.program_id(1)
    last = pl.num_programs(1) - 1
    R = q_ref.shape[0]
    rows_per_map = B_HEADS * n_new
    L = page_size * B_HEADS

    @pl.when(j == 0)
    def _():
        m_scr[...] = jnp.full(m_scr.shape, NEG_INF, F32)
        l_scr[...] = jnp.zeros(l_scr.shape, F32)
        acc_scr[...] = jnp.zeros(acc_scr.shape, F32)

    q = q_ref[...]

    def row_head_query(width):
        r = lax.broadcasted_iota(jnp.int32, (R, width), 0) % rows_per_map
        return r // n_new, r % n_new

    col = lax.broadcasted_iota(jnp.int32, (R, L), 1)
    head, qi = row_head_query(L)
    own_head = (col % B_HEADS) == head

    def page_scores(pi, with_bias):
        k2 = k_refs[pi][...].reshape(L, 2 * B_QK_DIM).astype(BF16)
        s = _dot_nt(q, k2)
        if with_bias:
            dist = qi + page_size - col // B_HEADS
            s = s + _rel_bias_minus_far(dist, lambda b: rbrow_ref[:, b:b + 1])
        return jnp.where(own_head, s, NEG_INF)

    def attend(scores, values):
        m_prev = m_scr[...]
        m_new = m_prev
        for s in scores:
            m_new = jnp.maximum(m_new, jnp.max(s, axis=-1, keepdims=True))
        alpha = jnp.exp(m_prev - m_new)
        ps = [jnp.exp(s - m_new) for s in scores]
        l_new = alpha * l_scr[...]
        acc = alpha * acc_scr[...]
        for p in ps:
            l_new = l_new + jnp.sum(p, axis=-1, keepdims=True)
        for p, v in zip(ps, values):
            acc = acc + _dot(p.astype(BF16), v)
        l_scr[...] = l_new
        acc_scr[...] = acc
        m_scr[...] = m_new

    def page_values():
        return [v_refs[pi][...].reshape(L, B_V_DIM).astype(BF16) for pi in range(P)]

    @pl.when(j != last)
    def _():
        attend([page_scores(pi, False) for pi in range(P)], page_values())

    @pl.when(j == last)
    def _():
        scores = [page_scores(pi, pi == P - 1) for pi in range(P)]
        kn = kn_ref[...]
        s = _dot_nt(q, kn)
        nl = kn.shape[0]
        ncol = lax.broadcasted_iota(jnp.int32, (R, nl), 1)
        nhead, nqi = row_head_query(nl)
        tok = ncol // B_HEADS
        s = s + _rel_bias_minus_far(nqi - tok, lambda b: rbrow_ref[:, b:b + 1])
        s = jnp.where((ncol % B_HEADS == nhead) & (tok <= nqi), s, NEG_INF)
        attend(scores + [s], page_values() + [vn_ref[...]])
        lam = _lambda(lq1_ref, lk1_ref, lq2_ref, lk2_ref, lam_init)
        o = acc_scr[...] / l_scr[...]
        o = o[0:rows_per_map] - lam * o[rows_per_map:2 * rows_per_map]
        o_ref[...] = _sub_norm(o, subln_ref[...], lam_init).astype(o_ref.dtype)


def _decode_attention(page_table, qd, kn, vn, rbrow, lam_p, subln, cache_k, cache_v, layer, lam_init, n_new,
                      pages_per_step):
    nseq, n_pages = page_table.shape
    page_size = cache_k.shape[2]
    P = pages_per_step
    n_steps = n_pages // P
    R = qd.shape[1]
    hd = 2 * B_QK_DIM
    vec = lambda n: pl.BlockSpec((1, n), lambda b, j, pt: (0, 0))

    def page_spec(pi, width):
        return pl.BlockSpec((None, None, page_size, B_HEADS, width),
                            lambda b, j, pt: (layer, pt[b, j * P + pi], 0, 0, 0))

    kern = functools.partial(_decode_kernel, pages_per_step=P, page_size=page_size, n_new=n_new, lam_init=lam_init)
    grid_spec = pltpu.PrefetchScalarGridSpec(
        num_scalar_prefetch=1,
        grid=(nseq, n_steps),
        in_specs=[pl.BlockSpec((None, R, hd), lambda b, j, pt: (b, 0, 0)),
                  pl.BlockSpec((None, kn.shape[1], hd), lambda b, j, pt: (b, 0, 0)),
                  pl.BlockSpec((None, vn.shape[1], B_V_DIM), lambda b, j, pt: (b, 0, 0)),
                  pl.BlockSpec(rbrow.shape, lambda b, j, pt: (0, 0)),
                  vec(B_QK_DIM), vec(B_QK_DIM), vec(B_QK_DIM), vec(B_QK_DIM), vec(B_V_DIM)]
                 + [page_spec(pi, hd) for pi in range(P)] + [page_spec(pi, B_V_DIM) for pi in range(P)],
        out_specs=pl.BlockSpec((None, R // 2, B_V_DIM), lambda b, j, pt: (b, 0, 0)),
        scratch_shapes=[pltpu.VMEM((R, 1), F32), pltpu.VMEM((R, 1), F32), pltpu.VMEM((R, B_V_DIM), F32)],
    )
    return pl.pallas_call(
        kern,
        grid_spec=grid_spec,
        out_shape=jax.ShapeDtypeStruct((nseq, R // 2, B_V_DIM), BF16),
        compiler_params=_cparams("arbitrary", "arbitrary"),
        name="diff_paged_decode",
    )(page_table, qd, kn, vn, rbrow, *lam_p, subln, *([cache_k] * P), *([cache_v] * P))


def _merge_kernel(ya_ref, yb_ref, ga_ref, gb_ref, wa_ref, wb_ref, o_ref):
    ma = _dot(ya_ref[...], wa_ref[...])
    mb = _dot(yb_ref[...], wb_ref[...])
    o_ref[...] = (_sigmoid(ga_ref[...]) * ma + _sigmoid(gb_ref[...]) * mb).astype(o_ref.dtype)


def _merge(ya, yb, gates, wa, wb, tm):
    m = ya.shape[0]
    d = wa.shape[1]
    return pl.pallas_call(
        _merge_kernel,
        grid=(m // tm,),
        in_specs=[pl.BlockSpec((tm, ya.shape[1]), lambda i: (i, 0)),
                  pl.BlockSpec((tm, yb.shape[1]), lambda i: (i, 0)),
                  pl.BlockSpec((tm, d), lambda i: (i, 0)),
                  pl.BlockSpec((tm, d), lambda i: (i, 1)),
                  pl.BlockSpec(wa.shape, lambda i: (0, 0)),
                  pl.BlockSpec(wb.shape, lambda i: (0, 0))],
        out_specs=pl.BlockSpec((tm, d), lambda i: (i, 0)),
        out_shape=jax.ShapeDtypeStruct((m, d), BF16),
        compiler_params=_cparams("arbitrary"),
        name="branch_merge",
    )(ya, yb, gates, gates, wa, wb)


def _out_router_kernel(mg_ref, x_ref, g1_ref, gain_ref, scale_ref, shift_ref, wo_ref, wrh_ref, wrm_ref, br_ref,
                       x1_ref, h2_ref, meta_ref, cnt_ref, carry_scr):
    @pl.when(pl.program_id(0) == 0)
    def _():
        carry_scr[...] = jnp.zeros(carry_scr.shape, F32)

    x1 = x_ref[...] + g1_ref[...] * _dot(mg_ref[...], wo_ref[...])
    x1_ref[...] = x1
    h2 = _modnorm(x1, gain_ref[...], scale_ref[...], shift_ref[...])
    h2_ref[...] = h2
    hi, mid, _ = _split3(h2)
    logit = _dot(hi, wrh_ref[...]) + _dot(hi, wrm_ref[...]) + _dot(mid, wrh_ref[...]) + br_ref[...]
    lane = lax.broadcasted_iota(jnp.int32, logit.shape, 1)
    big = jnp.int32(2 ** 30)

    def first_max(mask):
        v = jnp.max(jnp.where(mask, logit, NEG_INF), axis=-1, keepdims=True)
        idx = jnp.min(jnp.where(mask & (logit == v), lane, big), axis=-1, keepdims=True)
        return v, idx

    coarse = lane < N_GROUPS
    vc, g_idx = first_max(coarse)
    p_g = 1.0 / jnp.sum(jnp.where(coarse, jnp.exp(logit - vc), 0.0), axis=-1, keepdims=True)
    lo = N_GROUPS + g_idx * EXPERTS_PER_GROUP
    fine = (lane >= lo) & (lane < lo + EXPERTS_PER_GROUP)
    v1, i1 = first_max(fine)
    v2, i2 = first_max(fine & (lane != i1))
    e2 = jnp.exp(v2 - v1)
    p1 = 1.0 / (1.0 + e2)
    tm = logit.shape[0]
    member = ((lane == i1) | (lane == i2)).astype(BF16)
    rr = lax.broadcasted_iota(jnp.int32, (tm, tm), 0)
    cc = lax.broadcasted_iota(jnp.int32, (tm, tm), 1)
    rank = carry_scr[...] + _dot((cc < rr).astype(BF16), member)
    carry = carry_scr[...] + jnp.sum(member.astype(F32), axis=0, keepdims=True)
    carry_scr[...] = carry
    cnt_ref[...] = carry
    pick = lambda mask, x: jnp.sum(jnp.where(mask, x, 0.0), axis=-1, keepdims=True)
    fields = [(i1 - N_GROUPS).astype(F32), (i2 - N_GROUPS).astype(F32), pick(lane == i1, rank), pick(lane == i2, rank),
              p_g * p1, p_g * (e2 * p1)]
    meta = jnp.zeros(logit.shape, F32)
    for n, f in enumerate(fields):
        meta = jnp.where(lane == n, f, meta)
    meta_ref[...] = meta


META_E1, META_E2, META_R1, META_R2, META_W1, META_W2 = range(6)


def _out_router(mg, x, g1, gain, scale, shift, wo, wrh, wrm, br, tm):
    m, d = x.shape
    full = lambda a: pl.BlockSpec(a.shape, lambda i: (0, 0))
    row = lambda n: pl.BlockSpec((tm, n), lambda i: (i, 0))
    return pl.pallas_call(
        _out_router_kernel,
        grid=(m // tm,),
        in_specs=[row(d), row(d), _mod_spec(g1, tm), full(gain), _mod_spec(scale, tm), _mod_spec(shift, tm),
                  full(wo), full(wrh), full(wrm), full(br)],
        out_specs=[row(d), row(d), row(ROUTER_LANES), pl.BlockSpec((1, ROUTER_LANES), lambda i: (0, 0))],
        out_shape=[jax.ShapeDtypeStruct((m, d), F32), jax.ShapeDtypeStruct((m, d), F32),
                   jax.ShapeDtypeStruct((m, ROUTER_LANES), F32), jax.ShapeDtypeStruct((1, ROUTER_LANES), F32)],
        scratch_shapes=[pltpu.VMEM((1, ROUTER_LANES), F32)],
        compiler_params=_cparams("arbitrary"),
        name="out_proj_router",
    )(mg, x, g1, gain, scale, shift, wo, wrh, wrm, br)


EXPERT_TILE = 256


def _row_copy(src_ref, src_row, dst_ref, dst_row, sem):
    return pltpu.make_async_copy(src_ref.at[pl.ds(src_row, 1)], dst_ref.at[pl.ds(dst_row, 1)], sem)


def _dispatch_kernel(slots_ref, h_ref, xs_in_ref, xs_ref, sem):
    tb = h_ref.shape[0]
    base = pl.program_id(0) * tb

    def issue(t, c):
        for k in range(2):
            _row_copy(h_ref, t, xs_ref, slots_ref[2 * (base + t) + k], sem).start()
        return c

    def drain(t, c):
        for k in range(2):
            _row_copy(h_ref, 0, xs_ref, 0, sem).wait()
        return c

    lax.fori_loop(0, tb, issue, 0)
    lax.fori_loop(0, tb, drain, 0)


def _moe_dispatch(h2, slots, n_rows, tb):
    m, d = h2.shape
    grid_spec = pltpu.PrefetchScalarGridSpec(
        num_scalar_prefetch=1,
        grid=(m // tb,),
        in_specs=[pl.BlockSpec((tb, d), lambda i, s: (i, 0)), pl.BlockSpec(memory_space=pl.ANY)],
        out_specs=pl.BlockSpec(memory_space=pl.ANY),
        scratch_shapes=[pltpu.SemaphoreType.DMA(())],
    )
    return pl.pallas_call(
        _dispatch_kernel,
        grid_spec=grid_spec,
        out_shape=jax.ShapeDtypeStruct((n_rows, d), F32),
        input_output_aliases={2: 0},
        compiler_params=_cparams("arbitrary"),
        name="moe_dispatch",
    )(slots, h2, jnp.zeros((n_rows, d), F32))


def _expert_kernel(te_ref, nu_ref, x_ref, wg_ref, wu_ref, wd_ref, y_ref):
    t = pl.program_id(0)

    @pl.when(t < nu_ref[0])
    def _():
        x = x_ref[...].astype(BF16)
        hg = _dot(x, wg_ref[...].astype(BF16))
        hu = _dot(x, wu_ref[...].astype(BF16))
        act = _silu(hg) * hu
        y_ref[...] = _dot(act.astype(BF16), wd_ref[...].astype(BF16))

    @pl.when(t >= nu_ref[0])
    def _():
        y_ref[...] = jnp.zeros(y_ref.shape, F32)


def _moe_experts(xs, tile_expert, n_used, wg, wu, wd):
    n_rows, d = xs.shape
    f = wg.shape[2]
    n_tiles = n_rows // EXPERT_TILE
    used = lambda t, nu: jnp.minimum(t, nu[0] - 1)
    grid_spec = pltpu.PrefetchScalarGridSpec(
        num_scalar_prefetch=2,
        grid=(n_tiles,),
        in_specs=[pl.BlockSpec((EXPERT_TILE, d), lambda t, te, nu: (used(t, nu), 0)),
                  pl.BlockSpec((None, d, f), lambda t, te, nu: (te[t], 0, 0)),
                  pl.BlockSpec((None, d, f), lambda t, te, nu: (te[t], 0, 0)),
                  pl.BlockSpec((None, f, d), lambda t, te, nu: (te[t], 0, 0))],
        out_specs=pl.BlockSpec((EXPERT_TILE, d), lambda t, te, nu: (t, 0)),
    )
    return pl.pallas_call(
        _expert_kernel,
        grid_spec=grid_spec,
        out_shape=jax.ShapeDtypeStruct((n_rows, d), F32),
        compiler_params=_cparams("arbitrary"),
        name="moe_experts",
    )(tile_expert, n_used, xs, wg, wu, wd)


def _combine_kernel(slots_ref, ys_ref, meta_ref, x1_ref, g2_ref, o_ref, ybuf, sem):
    tb = x1_ref.shape[0]
    base = pl.program_id(0) * tb

    def issue(t, c):
        for k in range(2):
            _row_copy(ys_ref, slots_ref[2 * (base + t) + k], ybuf.at[k], t, sem).start()
        return c

    def drain(t, c):
        for k in range(2):
            _row_copy(ys_ref, 0, ybuf.at[k], 0, sem).wait()
        return c

    lax.fori_loop(0, tb, issue, 0)
    lax.fori_loop(0, tb, drain, 0)
    meta = meta_ref[...]
    lane = lax.broadcasted_iota(jnp.int32, meta.shape, 1)
    w1 = jnp.sum(jnp.where(lane == META_W1, meta, 0.0), axis=-1, keepdims=True)
    w2 = jnp.sum(jnp.where(lane == META_W2, meta, 0.0), axis=-1, keepdims=True)
    o_ref[...] = x1_ref[...] + g2_ref[...] * (w1 * ybuf[0] + w2 * ybuf[1])


def _moe_combine(ys, slots, meta, x1, g2, tb):
    m, d = x1.shape
    grid_spec = pltpu.PrefetchScalarGridSpec(
        num_scalar_prefetch=1,
        grid=(m // tb,),
        in_specs=[pl.BlockSpec(memory_space=pl.ANY),
                  pl.BlockSpec((tb, ROUTER_LANES), lambda i, s: (i, 0)),
                  pl.BlockSpec((tb, d), lambda i, s: (i, 0)),
                  _mod_spec(g2, tb)],
        out_specs=pl.BlockSpec((tb, d), lambda i, s: (i, 0)),
        scratch_shapes=[pltpu.VMEM((2, tb, d), F32), pltpu.SemaphoreType.DMA(())],
    )
    return pl.pallas_call(
        _combine_kernel,
        grid_spec=grid_spec,
        out_shape=jax.ShapeDtypeStruct((m, d), F32),
        compiler_params=_cparams("arbitrary"),
        name="moe_combine",
    )(slots, ys, meta, x1, g2)


def _moe_sorted(h2, meta, cnt, x1, g2, wg, wu, wd, tb):
    m = h2.shape[0]
    n_tiles = (2 * m) // EXPERT_TILE + N_EXPERTS
    counts = cnt[0, N_GROUPS:N_GROUPS + N_EXPERTS].astype(jnp.int32)
    padded = (counts + EXPERT_TILE - 1) // EXPERT_TILE * EXPERT_TILE
    seg_end = jnp.cumsum(padded)
    seg_start = seg_end - padded
    n_used = (seg_end[-1] // EXPERT_TILE).astype(jnp.int32).reshape(1)
    tile = jnp.minimum(jnp.arange(n_tiles, dtype=jnp.int32), n_used[0] - 1)
    tile_expert = jnp.sum(seg_end[None, :] // EXPERT_TILE <= tile[:, None], axis=1).astype(jnp.int32)
    e12 = meta[:, META_E1:META_E2 + 1].astype(jnp.int32)
    r12 = meta[:, META_R1:META_R2 + 1].astype(jnp.int32)
    slots = (seg_start[e12] + r12).reshape(-1)
    xs = _moe_dispatch(h2, slots, n_tiles * EXPERT_TILE, tb)
    ys = _moe_experts(xs, tile_expert, n_used, wg, wu, wd)
    return _moe_combine(ys, slots, meta, x1, g2, tb)


def _pick(m, prefs):
    for t in prefs:
        if m % t == 0:
            return t
    return m


def _prep_layer_params(l, w_in, mu_shift, w0, w_decay_up, a0, w_iclr_up, w_gate_up, k_k, k_a, r_k, ln_x_w, ln_x_b,
                       q_norm, k_norm, lambda_q1, lambda_k1, lambda_q2, lambda_k2, subln, w_branch_a, w_branch_b,
                       w_out, w_router_coarse, b_router_coarse, w_router_fine, b_router_fine,
                       w_exp_gate, w_exp_up, w_exp_down, norm1, norm2):
    d = w_in.shape[1]
    wi = w_in[l]
    c_qkv = A_COLS + 2 * B_QK_WIDTH + B_V_WIDTH
    w_lora = jnp.zeros((LORA_IN, 3 * A_WIDTH), F32)
    w_lora = w_lora.at[0:DECAY_LORA, 0:A_WIDTH].set(w_decay_up[l])
    w_lora = w_lora.at[DECAY_LORA:DECAY_LORA + ICLR_LORA, A_WIDTH:2 * A_WIDTH].set(w_iclr_up[l])
    w_lora = w_lora.at[DECAY_LORA + ICLR_LORA:, 2 * A_WIDTH:].set(w_gate_up[l])
    lane = jnp.arange(LANES)
    sel64 = (lane[:, None] // A_HEAD_DIM == lane[None, :] // A_HEAD_DIM).astype(BF16)
    w_r = jnp.zeros((d, ROUTER_LANES), F32)
    w_r = w_r.at[:, 0:N_GROUPS].set(w_router_coarse[l])
    w_r = w_r.at[:, N_GROUPS:N_GROUPS + N_EXPERTS].set(
        jnp.transpose(w_router_fine[l], (1, 0, 2)).reshape(d, N_EXPERTS))
    w_rh = w_r.astype(BF16)
    w_rm = (w_r - w_rh.astype(F32)).astype(BF16)
    b_r = jnp.zeros((1, ROUTER_LANES), F32)
    b_r = b_r.at[0, 0:N_GROUPS].set(b_router_coarse[l])
    b_r = b_r.at[0, N_GROUPS:N_GROUPS + N_EXPERTS].set(b_router_fine[l].reshape(-1))
    row = lambda a: a.reshape(1, -1)
    return dict(
        w_in_a=wi[:, 0:A_COLS].astype(BF16), w_in_qkv=wi[:, A_COLS:c_qkv].astype(BF16),
        w_in_g=wi[:, c_qkv:].astype(BF16),
        mu_shift=row(mu_shift[l]), w_lora=w_lora.astype(BF16), w0=row(w0[l]), a0=row(a0[l]), k_k=row(k_k[l]),
        k_a=row(k_a[l]), r_k=row(r_k[l]), ln_x_w=row(ln_x_w[l]), ln_x_b=row(ln_x_b[l]), sel64=sel64,
        q_gain=row(jnp.tile(q_norm[l].reshape(-1), B_HEADS)), k_gain=row(jnp.tile(k_norm[l].reshape(-1), B_HEADS)),
        lam=(row(lambda_q1[l]), row(lambda_k1[l]), row(lambda_q2[l]), row(lambda_k2[l])), subln=row(subln[l]),
        w_a=w_branch_a[l].astype(BF16), w_b=w_branch_b[l].astype(BF16), w_out=w_out[l].astype(BF16),
        w_rh=w_rh, w_rm=w_rm, b_r=b_r,
        w_eg=w_exp_gate[l], w_eu=w_exp_up[l], w_ed=w_exp_down[l],
        norm1=row(norm1[l]), norm2=row(norm2[l]))


def _trunk_rows(x2, mods, p, tm):
    sh1, sc1 = mods[0], mods[1]
    tm_a = _pick(x2.shape[0], (tm,))
    ua = _norm_proj(x2, p["norm1"], sc1, sh1, p["w_in_a"], tm_a, A_COLS // 2)
    uqkv = _norm_proj(x2, p["norm1"], sc1, sh1, p["w_in_qkv"], tm_a, 1024)
    ug = _norm_proj(x2, p["norm1"], sc1, sh1, p["w_in_g"], tm_a, 1024)
    return ua, uqkv, ug


def _tail_rows(x2, ya, yb, ug, mods, p, tm):
    _, _, g1, sh2, sc2, g2 = mods
    m = x2.shape[0]
    mg = _merge(ya, yb, ug, p["w_a"], p["w_b"], _pick(m, (tm,)))
    tb = _pick(m, (256, 128))
    x1, h2, meta, cnt = _out_router(mg, x2, g1, p["norm2"], sc2, sh2, p["w_out"], p["w_rh"], p["w_rm"], p["b_r"], tb)
    return _moe_sorted(h2, meta, cnt, x1, g2, p["w_eg"], p["w_eu"], p["w_ed"], tb)


def kernel(x_prompt, x_sample, c_prompt, c_sample, cache_k, cache_v, state_wkv, state_shift, page_table, rel_bias, w_ada, b_ada, norm1, norm2, w_in, mu_shift, w0, w_decay_up, a0, w_iclr_up, w_gate_up, k_k, k_a, r_k, ln_x_w, ln_x_b, q_norm, k_norm, lambda_q1, lambda_k1, lambda_q2, lambda_k2, subln, w_branch_a, w_branch_b, w_out, w_router_coarse, b_router_coarse, w_router_fine, b_router_fine, w_exp_gate, w_exp_up, w_exp_down):
    depth = w_in.shape[0]
    bp, tp, d = x_prompt.shape
    bs, ts, _ = x_sample.shape
    n_new_pad = 8
    kv_new_rows = LANES // B_HEADS
    yp = x_prompt.reshape(bp * tp, d)
    ys = x_sample.reshape(bs * ts, d)
    c_all = jnp.concatenate([c_prompt, c_sample], axis=0)
    c_rows = -(-c_all.shape[0] // 8) * 8
    c_all = jnp.pad(c_all, ((0, c_rows - c_all.shape[0]), (0, 0)))
    outs = [[] for _ in range(8)]
    for l in range(depth):
        lam_init = 0.8 - 0.6 * math.exp(-0.3 * l)
        p = _prep_layer_params(l, w_in, mu_shift, w0, w_decay_up, a0, w_iclr_up, w_gate_up, k_k, k_a, r_k, ln_x_w,
                               ln_x_b, q_norm, k_norm, lambda_q1, lambda_k1, lambda_q2, lambda_k2, subln,
                               w_branch_a, w_branch_b, w_out, w_router_coarse, b_router_coarse, w_router_fine,
                               b_router_fine, w_exp_gate, w_exp_up, w_exp_down, norm1, norm2)
        ada = _ada(c_all, w_ada[l], b_ada[l])
        mods_p = [jnp.repeat(a, tp, axis=0) if bp > 1 else a for a in jnp.split(ada[0:bp], 6, axis=-1)]
        ua, uqkv, ug = _trunk_rows(yp, mods_p, p, 512)
        ua3 = ua.reshape(bp, tp, A_COLS)
        ya, wkv_p = _rwkv(ua3, jnp.zeros((bp, 1, A_COLS), F32),
                          jnp.zeros((bp, A_HEADS, A_HEAD_DIM, A_HEAD_DIM), F32), p, 64, None)
        qt2, kf, kb, vf, vt = _qk_prep(uqkv, p["q_gain"], p["k_gain"], p["sel64"], 512, True)
        t_blk = _pick(tp, (512, 256, 128))
        yb = jnp.concatenate(
            [_flash_prompt(qt2[:, :, b * tp:(b + 1) * tp], kb[b * tp:(b + 1) * tp], vt[:, b * tp:(b + 1) * tp],
                           rel_bias, p["lam"], p["subln"], lam_init, t_blk) for b in range(bp)], axis=0)
        yp = _tail_rows(yp, ya.reshape(bp * tp, A_WIDTH), yb, ug, mods_p, p, 512)
        outs[0].append(kf.reshape(bp, tp, B_HEADS, 2 * B_QK_DIM))
        outs[1].append(vf.reshape(bp, tp, B_HEADS, B_V_DIM))
        outs[2].append(wkv_p)
        outs[3].append(ua3[:, -1])
        mods_s = [jnp.repeat(a, ts, axis=0) for a in jnp.split(ada[bp:bp + bs], 6, axis=-1)]
        ua, uqkv, ug = _trunk_rows(ys, mods_s, p, 128)
        ua3 = ua.reshape(bs, ts, A_COLS)
        ua_pad = jnp.pad(ua3, ((0, 0), (0, n_new_pad - ts), (0, 0)))
        ya, wkv_s = _rwkv(ua_pad, state_shift[l][:, None, :], state_wkv[l], p, n_new_pad, ts)
        ya = ya[:, 0:ts].reshape(bs * ts, A_WIDTH)
        q2, kf, kb, vf, vb = _qk_prep(uqkv, p["q_gain"], p["k_gain"], p["sel64"], _pick(bs * ts, (128,)), False)
        qd = q2.reshape(2, bs, ts, B_HEADS, 2 * B_QK_DIM).transpose(1, 0, 3, 2, 4).reshape(
            bs, 2 * B_HEADS * ts, 2 * B_QK_DIM)
        kn = jnp.pad(kb.reshape(bs, ts * B_HEADS, 2 * B_QK_DIM), ((0, 0), (0, (kv_new_rows - ts) * B_HEADS), (0, 0)))
        vn = jnp.pad(vb.reshape(bs, ts * B_HEADS, B_V_DIM), ((0, 0), (0, (kv_new_rows - ts) * B_HEADS), (0, 0)))
        rbrow = jnp.tile(jnp.repeat(rel_bias.T, ts, axis=0), (2, 1))
        od = _decode_attention(page_table, qd, kn, vn, rbrow, p["lam"], p["subln"], cache_k, cache_v, l, lam_init,
                               ts, _pick(page_table.shape[1], (8, 4, 2, 1)))
        yb = od.reshape(bs, B_HEADS, ts, B_V_DIM).transpose(0, 2, 1, 3).reshape(bs * ts, B_V_WIDTH)
        ys = _tail_rows(ys, ya, yb, ug, mods_s, p, 128)
        outs[4].append(kf.reshape(bs, ts, B_HEADS, 2 * B_QK_DIM))
        outs[5].append(vf.reshape(bs, ts, B_HEADS, B_V_DIM))
        outs[6].append(wkv_s)
        outs[7].append(ua3[:, -1])
    st = [jnp.stack(o) for o in outs]
    return (yp.reshape(bp, tp, d), ys.reshape(bs, ts, d), st[0], st[1], st[2], st[3], st[4], st[5], st[6], st[7])
```

```python
import functools
import math

import jax
import jax.numpy as jnp
from jax import lax
from jax.experimental import pallas as pl
from jax.experimental.pallas import tpu as pltpu

F32 = jnp.float32
BF16 = jnp.bfloat16

A_HEADS = 16
A_HEAD_DIM = 64
A_WIDTH = A_HEADS * A_HEAD_DIM
DECAY_LORA = 64
ICLR_LORA = 64
GATE_LORA = 128
LORA_IN = DECAY_LORA + ICLR_LORA + GATE_LORA
A_COLS = 3 * A_WIDTH + LORA_IN
GN_EPS = 64e-5
B_HEADS = 8
B_QK_DIM = 64
B_V_DIM = 2 * B_QK_DIM
B_QK_WIDTH = B_HEADS * 2 * B_QK_DIM
B_V_WIDTH = B_HEADS * B_V_DIM
NUM_BUCKETS = 32
MAX_DISTANCE = 128
N_GROUPS = 4
EXPERTS_PER_GROUP = 8
N_EXPERTS = N_GROUPS * EXPERTS_PER_GROUP
NORM_EPS = 1e-6
NEG_INF = -1e30

LANES = 128
ROUTER_LANES = LANES
VMEM_LIMIT = 56 * 1024 * 1024


def _cparams(*sem):
    return pltpu.CompilerParams(dimension_semantics=sem, vmem_limit_bytes=VMEM_LIMIT)


def _dot(a, b):
    return jnp.dot(a, b, preferred_element_type=F32)


def _dot_nt(a, b):
    return lax.dot_general(a, b, (((1,), (1,)), ((), ())), preferred_element_type=F32)


def _dot_tn(a, b):
    return lax.dot_general(a, b, (((0,), (0,)), ((), ())), preferred_element_type=F32)


def _split3(x):
    hi = x.astype(BF16)
    r1 = x - hi.astype(F32)
    mid = r1.astype(BF16)
    lo = (r1 - mid.astype(F32)).astype(BF16)
    return hi, mid, lo


def _dot_sel_rhs(x, sel):
    hi, mid, lo = _split3(x)
    return _dot(hi, sel) + _dot(mid, sel) + _dot(lo, sel)


def _dot_sel_lhs(sel, x):
    hi, mid, lo = _split3(x)
    return _dot(sel, hi) + _dot(sel, mid) + _dot(sel, lo)


def _group_sum(x, sel):
    parts = [_dot_sel_rhs(x[:, t * LANES:(t + 1) * LANES], sel) for t in range(x.shape[1] // LANES)]
    return parts[0] if len(parts) == 1 else jnp.concatenate(parts, axis=1)


def _sigmoid(x):
    return 1.0 / (1.0 + jnp.exp(-x))


def _silu(x):
    return x * _sigmoid(x)


def _rel_bias_minus_far(dist, table_fn):
    n = jnp.maximum(dist, 0)
    max_exact = NUM_BUCKETS // 2
    nf = jnp.maximum(n, 1).astype(F32)
    large = max_exact + (jnp.log(nf / max_exact) / math.log(MAX_DISTANCE / max_exact)
                         * (NUM_BUCKETS - max_exact)).astype(jnp.int32)
    large = jnp.minimum(large, NUM_BUCKETS - 1)
    bucket = jnp.where(n < max_exact, n, large)
    far = table_fn(NUM_BUCKETS - 1)
    out = jnp.zeros(dist.shape, F32)
    for b in range(NUM_BUCKETS - 1):
        out = jnp.where(bucket == b, table_fn(b) - far, out)
    return out


def _ada_kernel(c_ref, w_ref, b_ref, o_ref):
    s = _silu(c_ref[...])
    o_ref[...] = _dot(s.astype(BF16), w_ref[...].astype(BF16)) + b_ref[...]


def _ada(c, w, b):
    m, d = c.shape
    n = w.shape[1]
    tn = 1024
    return pl.pallas_call(
        _ada_kernel,
        grid=(n // tn,),
        in_specs=[pl.BlockSpec((m, d), lambda j: (0, 0)),
                  pl.BlockSpec((d, tn), lambda j: (0, j)),
                  pl.BlockSpec((1, tn), lambda j: (0, j))],
        out_specs=pl.BlockSpec((m, tn), lambda j: (0, j)),
        out_shape=jax.ShapeDtypeStruct((m, n), F32),
        compiler_params=_cparams("arbitrary"),
        name="ada_modulation",
    )(c, w, b.reshape(1, n))


def _modnorm(x, gain, scale, shift):
    ms = jnp.mean(x * x, axis=-1, keepdims=True)
    h = x * lax.rsqrt(ms + NORM_EPS) * gain
    return h * (1.0 + scale) + shift


def _norm_proj_kernel(x_ref, gain_ref, scale_ref, shift_ref, w_ref, o_ref, h_ref):
    @pl.when(pl.program_id(1) == 0)
    def _():
        h_ref[...] = _modnorm(x_ref[...], gain_ref[...], scale_ref[...], shift_ref[...]).astype(BF16)

    o_ref[...] = _dot(h_ref[...], w_ref[...])


def _proj_kernel(h_ref, w_ref, o_ref):
    o_ref[...] = _dot(h_ref[...], w_ref[...])


def _proj(h, w, tm, tn):
    m, d = h.shape
    n = w.shape[1]
    return pl.pallas_call(
        _proj_kernel,
        grid=(m // tm, n // tn),
        in_specs=[pl.BlockSpec((tm, d), lambda i, j: (i, 0)), pl.BlockSpec((d, tn), lambda i, j: (0, j))],
        out_specs=pl.BlockSpec((tm, tn), lambda i, j: (i, j)),
        out_shape=jax.ShapeDtypeStruct((m, n), F32),
        compiler_params=_cparams("arbitrary", "arbitrary"),
        name="in_proj",
    )(h, w)


def _mod_spec(mod, tm):
    if mod.shape[0] == 1:
        return pl.BlockSpec((1, mod.shape[1]), lambda i, *_: (0, 0))
    return pl.BlockSpec((tm, mod.shape[1]), lambda i, *_: (i, 0))


def _norm_proj(x, gain, scale, shift, w, tm, tn):
    m, d = x.shape
    n = w.shape[1]
    return pl.pallas_call(
        _norm_proj_kernel,
        grid=(m // tm, n // tn),
        in_specs=[pl.BlockSpec((tm, d), lambda i, j: (i, 0)),
                  pl.BlockSpec((1, d), lambda i, j: (0, 0)),
                  _mod_spec(scale, tm), _mod_spec(shift, tm),
                  pl.BlockSpec((d, tn), lambda i, j: (0, j))],
        out_specs=[pl.BlockSpec((tm, tn), lambda i, j: (i, j)), pl.BlockSpec((tm, d), lambda i, j: (i, 0))],
        out_shape=[jax.ShapeDtypeStruct((m, n), F32), jax.ShapeDtypeStruct((m, d), BF16)],
        compiler_params=_cparams("arbitrary", "arbitrary"),
        name="norm_in_proj",
    )(x, gain, scale, shift, w)


def _rwkv_kernel(ua_ref, shift0_ref, s0_ref, mu_ref, wl_ref, w0_ref, a0_ref, kk_ref, ka_ref, rk_ref,
                 lnw_ref, lnb_ref, sel_ref, y_ref, sout_ref, state_scr, prev_scr, yh_scr, *, chunk, n_valid):
    c = pl.program_id(1)
    n_chunks = pl.num_programs(1)
    C = chunk
    W = A_WIDTH

    @pl.when(c == 0)
    def _():
        state_scr[...] = s0_ref[...]
        prev_scr[...] = shift0_ref[...]

    ua = ua_ref[...]
    row = lax.broadcasted_iota(jnp.int32, (C, 1), 0)
    prev = jnp.where(row == 0, prev_scr[...], pltpu.roll(ua, 1, axis=0))
    prev_scr[...] = ua[C - 1:C, :]
    us = ua + (prev - ua) * mu_ref[...]
    r = us[:, 0:W]
    kr = us[:, W:2 * W]
    vr = us[:, 2 * W:3 * W]
    lin = us[:, 3 * W:3 * W + LORA_IN]
    lane = lax.broadcasted_iota(jnp.int32, (1, LORA_IN), 1)
    act = jnp.where(lane < DECAY_LORA, jnp.tanh(lin),
                    jnp.where(lane < DECAY_LORA + ICLR_LORA, lin, _sigmoid(lin)))
    lora = _dot(act.astype(BF16), wl_ref[...])
    z = -(w0_ref[...] + lora[:, 0:W])
    softplus = jnp.maximum(z, 0.0) + jnp.log(1.0 + jnp.exp(-jnp.abs(z)))
    ld = -jnp.exp(-softplus - 0.5)
    iclr = _sigmoid(a0_ref[...] + lora[:, W:2 * W])
    gate_o = lora[:, 2 * W:3 * W]

    sel = sel_ref[...]
    kk = kr * kk_ref[...]
    kk = kk / jnp.maximum(jnp.sqrt(_group_sum(kk * kk, sel)), 1e-12)
    k_mod = kr * (1.0 + (iclr - 1.0) * ka_ref[...])
    if n_valid is not None:
        valid = (c * C + row) < n_valid
        ld = jnp.where(valid, ld, 0.0)
        kk = jnp.where(valid, kk, 0.0)
        k_mod = jnp.where(valid, k_mod, 0.0)
        vr = jnp.where(valid, vr, 0.0)

    ri = lax.broadcasted_iota(jnp.int32, (C, C), 0)
    ci = lax.broadcasted_iota(jnp.int32, (C, C), 1)
    cum = _dot_sel_lhs((ci <= ri).astype(BF16), ld)
    g_inv = jnp.exp(-cum)
    g_last = jnp.exp(cum[C - 1:C, :])
    at = -kk * jnp.exp(cum - ld)
    bt = kk * iclr * g_inv
    kt = k_mod * g_inv
    bl = bt * g_last
    kl = kt * g_last
    rt = r * jnp.exp(cum)
    eye = (ci == ri).astype(F32)
    n_double = max(int(math.ceil(math.log2(C))) - 1, 0)
    r2 = lax.broadcasted_iota(jnp.int32, (2 * C, 2 * C), 0)
    c2 = lax.broadcasted_iota(jnp.int32, (2 * C, 2 * C), 1)
    mask2 = ((r2 < C) & ((c2 % C) < r2)) | ((r2 >= C) & ((c2 % C) <= (r2 - C)))
    zeros_c = jnp.zeros((C, A_HEAD_DIM), F32)
    heads = range(A_HEADS)

    def hsl(x, h):
        return x[:, h * A_HEAD_DIM:(h + 1) * A_HEAD_DIM]

    def stack(top, bottom):
        return jnp.concatenate([top, bottom], axis=0).astype(BF16)

    atb = [hsl(at, h).astype(BF16) for h in heads]
    ar = [stack(hsl(at, h), hsl(rt, h)) for h in heads]
    bk = [stack(hsl(bt, h), hsl(kt, h)) for h in heads]
    zv = [stack(zeros_c, hsl(vr, h)) for h in heads]
    bkl = [stack(hsl(bl, h), hsl(kl, h)) for h in heads]
    m2 = [jnp.where(mask2, _dot_nt(ar[h], bk[h]), 0.0) for h in heads]
    m2b = [m.astype(BF16) for m in m2]
    bvk = [_dot(m2b[h], zv[h]) for h in heads]
    p = [m[0:C, 0:C] for m in m2]
    t_inv = [eye + a for a in p]
    for _ in range(n_double):
        pb = [x.astype(BF16) for x in p]
        p = [_dot(x, x) for x in pb]
        t_inv = [t + _dot(t.astype(BF16), x.astype(BF16)) for t, x in zip(t_inv, p)]
    tb = [t.astype(BF16) for t in t_inv]
    w = [_dot(tb[h], atb[h]) for h in heads]
    z = [_dot(tb[h], bvk[h][0:C].astype(BF16)) for h in heads]
    s0 = [state_scr[h] for h in heads]
    ws = [_dot_nt(stack(w[h], hsl(rt, h)), s0[h].astype(BF16)) for h in heads]
    u = [ws[h][0:C] + z[h] for h in heads]
    y = [ws[h][C:2 * C] + bvk[h][C:2 * C] + _dot(m2[h][C:2 * C, 0:C].astype(BF16), u[h].astype(BF16))
         for h in heads]
    s_new = [s0[h] * hsl(g_last, h) + _dot_tn(stack(u[h], hsl(vr, h)), bkl[h]) for h in heads]
    for h in heads:
        yh_scr[:, h * A_HEAD_DIM:(h + 1) * A_HEAD_DIM] = y[h]
        state_scr[h] = s_new[h]

    y = yh_scr[...]
    inv_n = 1.0 / A_HEAD_DIM
    mu_y = _group_sum(y, sel) * inv_n
    dy = y - mu_y
    var = _group_sum(dy * dy, sel) * inv_n
    yn = dy * lax.rsqrt(var + GN_EPS) * lnw_ref[...] + lnb_ref[...]
    bonus = _group_sum(r * k_mod * rk_ref[...], sel) * vr
    y_ref[...] = ((yn + bonus) * gate_o).astype(y_ref.dtype)

    @pl.when(c == n_chunks - 1)
    def _():
        sout_ref[...] = state_scr[...]


def _rwkv(ua, shift0, s0, p, chunk, n_valid):
    b, t, _ = ua.shape
    n_chunks = t // chunk
    vec = lambda n: pl.BlockSpec((1, n), lambda i, j: (0, 0))
    kern = functools.partial(_rwkv_kernel, chunk=chunk, n_valid=n_valid)
    return pl.pallas_call(
        kern,
        grid=(b, n_chunks),
        in_specs=[pl.BlockSpec((None, chunk, A_COLS), lambda i, j: (i, j, 0)),
                  pl.BlockSpec((None, 1, A_COLS), lambda i, j: (i, 0, 0)),
                  pl.BlockSpec((None, A_HEADS, A_HEAD_DIM, A_HEAD_DIM), lambda i, j: (i, 0, 0, 0)),
                  vec(A_COLS),
                  pl.BlockSpec((LORA_IN, 3 * A_WIDTH), lambda i, j: (0, 0)),
                  vec(A_WIDTH), vec(A_WIDTH), vec(A_WIDTH), vec(A_WIDTH), vec(A_WIDTH), vec(A_WIDTH), vec(A_WIDTH),
                  pl.BlockSpec((LANES, LANES), lambda i, j: (0, 0))],
        out_specs=[pl.BlockSpec((None, chunk, A_WIDTH), lambda i, j: (i, j, 0)),
                   pl.BlockSpec((None, A_HEADS, A_HEAD_DIM, A_HEAD_DIM), lambda i, j: (i, 0, 0, 0))],
        out_shape=[jax.ShapeDtypeStruct((b, t, A_WIDTH), BF16),
                   jax.ShapeDtypeStruct((b, A_HEADS, A_HEAD_DIM, A_HEAD_DIM), F32)],
        scratch_shapes=[pltpu.VMEM((A_HEADS, A_HEAD_DIM, A_HEAD_DIM), F32),
                        pltpu.VMEM((1, A_COLS), F32),
                        pltpu.VMEM((chunk, A_WIDTH), F32)],
        compiler_params=_cparams("arbitrary", "arbitrary"),
        name="rwkv7_chunked",
    )(ua, shift0, s0, p["mu_shift"], p["w_lora"], p["w0"], p["a0"], p["k_k"], p["k_a"], p["r_k"],
      p["ln_x_w"], p["ln_x_b"], p["sel64"])


def _qk_prep_kernel(u_ref, qg_ref, kg_ref, sel_ref, q_ref, kf_ref, kb_ref, vf_ref, vb_ref, *, transposed):
    sel = sel_ref[...]
    wq = B_QK_WIDTH
    uq = u_ref[:, 0:wq]
    uk = u_ref[:, wq:2 * wq]
    uv = u_ref[:, 2 * wq:2 * wq + B_V_WIDTH]
    inv_n = 1.0 / B_QK_DIM
    q = uq * lax.rsqrt(_group_sum(uq * uq, sel) * inv_n + NORM_EPS) * qg_ref[...]
    k = uk * lax.rsqrt(_group_sum(uk * uk, sel) * inv_n + NORM_EPS) * kg_ref[...]
    q = q * (B_QK_DIM ** -0.5)
    kf_ref[...] = k
    kb_ref[...] = k.astype(BF16)
    vf_ref[...] = uv
    if transposed:
        qt = q.T
        ch = lax.broadcasted_iota(jnp.int32, (wq, 1), 0)
        first_map = (ch % (2 * B_QK_DIM)) < B_QK_DIM
        q_ref[0] = jnp.where(first_map, qt, 0.0).astype(BF16)
        q_ref[1] = jnp.where(first_map, 0.0, qt).astype(BF16)
        vb_ref[...] = uv.T.astype(BF16)
    else:
        ch = lax.broadcasted_iota(jnp.int32, (1, wq), 1)
        first_map = (ch % (2 * B_QK_DIM)) < B_QK_DIM
        q_ref[0] = jnp.where(first_map, q, 0.0).astype(BF16)
        q_ref[1] = jnp.where(first_map, 0.0, q).astype(BF16)
        vb_ref[...] = uv.astype(BF16)


def _qk_prep(u_qkv, qg, kg, sel, tm, transposed):
    m = u_qkv.shape[0]
    w = B_QK_WIDTH
    row = lambda n: pl.BlockSpec((tm, n), lambda i: (i, 0))
    if transposed:
        q_spec, q_shape = pl.BlockSpec((2, w, tm), lambda i: (0, 0, i)), (2, w, m)
        v_spec, v_shape = pl.BlockSpec((w, tm), lambda i: (0, i)), (w, m)
    else:
        q_spec, q_shape = pl.BlockSpec((2, tm, w), lambda i: (0, i, 0)), (2, m, w)
        v_spec, v_shape = row(w), (m, w)
    return pl.pallas_call(
        functools.partial(_qk_prep_kernel, transposed=transposed),
        grid=(m // tm,),
        in_specs=[row(3 * w),
                  pl.BlockSpec((1, w), lambda i: (0, 0)), pl.BlockSpec((1, w), lambda i: (0, 0)),
                  pl.BlockSpec((LANES, LANES), lambda i: (0, 0))],
        out_specs=[q_spec, row(w), row(w), row(w), v_spec],
        out_shape=[jax.ShapeDtypeStruct(q_shape, BF16), jax.ShapeDtypeStruct((m, w), F32),
                   jax.ShapeDtypeStruct((m, w), BF16), jax.ShapeDtypeStruct((m, w), F32),
                   jax.ShapeDtypeStruct(v_shape, BF16)],
        compiler_params=_cparams("arbitrary"),
        name="qk_norm",
    )(u_qkv, qg, kg, sel)


def _lambda(lq1_ref, lk1_ref, lq2_ref, lk2_ref, lam_init):
    s1 = jnp.sum(lq1_ref[...] * lk1_ref[...], axis=-1, keepdims=True)
    s2 = jnp.sum(lq2_ref[...] * lk2_ref[...], axis=-1, keepdims=True)
    return jnp.exp(s1) - jnp.exp(s2) + lam_init


def _sub_norm(o, subln, lam_init):
    ms = jnp.mean(o * o, axis=-1, keepdims=True)
    return o * lax.rsqrt(ms + NORM_EPS) * subln * (1.0 - lam_init)


def _flash_kernel(rb_ref, q_ref, k_ref, v_ref, lq1_ref, lk1_ref, lq2_ref, lk2_ref, subln_ref, o_ref,
                  m_scr, l_scr, acc_scr, bias_scr, *, t, lam_init):
    h = pl.program_id(0)
    i = pl.program_id(1)

    @pl.when(i == 0)
    def _():
        kk = lax.broadcasted_iota(jnp.int32, (t, t), 0)
        qq = lax.broadcasted_iota(jnp.int32, (t, t), 1)
        table = lambda b: rb_ref[b, h]
        bias_scr[0] = jnp.where(qq >= kk, _rel_bias_minus_far(qq - kk, table), NEG_INF)
        bias_scr[1] = _rel_bias_minus_far(qq - kk + t, table)

    m_scr[...] = jnp.full(m_scr.shape, NEG_INF, F32)
    l_scr[...] = jnp.zeros(l_scr.shape, F32)
    acc_scr[...] = jnp.zeros(acc_scr.shape, F32)

    def step(j, bias):
        start = pl.multiple_of(j * t, t)
        k = k_ref[pl.ds(start, t), :]
        sc = [_dot(k, q_ref[mp]) for mp in range(2)]
        ps = []
        for mp in range(2):
            x = sc[mp] if bias is None else sc[mp] + bias
            m_prev = m_scr[mp]
            m_new = jnp.maximum(m_prev, jnp.max(x, axis=0, keepdims=True))
            alpha = jnp.exp(m_prev - m_new)
            p = jnp.exp(x - m_new)
            l_scr[mp] = alpha * l_scr[mp] + jnp.sum(p, axis=0, keepdims=True)
            acc_scr[mp] = alpha * acc_scr[mp]
            m_scr[mp] = m_new
            ps.append(p.astype(BF16))
        v = v_ref[:, pl.ds(start, t)]
        for mp in range(2):
            acc_scr[mp] += _dot(v, ps[mp])

    def far_block(j, carry):
        step(j, None)
        return carry

    lax.fori_loop(0, i - 1, far_block, 0)

    @pl.when(i > 0)
    def _():
        step(i - 1, bias_scr[1])

    step(i, bias_scr[0])
    lam = _lambda(lq1_ref, lk1_ref, lq2_ref, lk2_ref, lam_init)
    o = acc_scr[0] / l_scr[0] - lam * (acc_scr[1] / l_scr[1])
    ms = jnp.mean(o * o, axis=0, keepdims=True)
    o = o * lax.rsqrt(ms + NORM_EPS) * subln_ref[...] * (1.0 - lam_init)
    o_ref[...] = o.T.astype(o_ref.dtype)


def _flash_prompt(qt2, kb, vt, rel_bias, lam_p, subln, lam_init, t):
    n_tok = kb.shape[0]
    nb = n_tok // t
    hd = 2 * B_QK_DIM
    vec = lambda n: pl.BlockSpec((1, n), lambda h, i: (0, 0))
    kern = functools.partial(_flash_kernel, t=t, lam_init=lam_init)
    return pl.pallas_call(
        kern,
        grid=(B_HEADS, nb),
        in_specs=[pl.BlockSpec(memory_space=pltpu.SMEM),
                  pl.BlockSpec((2, hd, t), lambda h, i: (0, h, i)),
                  pl.BlockSpec((n_tok, hd), lambda h, i: (0, h)),
                  pl.BlockSpec((B_V_DIM, n_tok), lambda h, i: (h, 0)),
                  vec(B_QK_DIM), vec(B_QK_DIM), vec(B_QK_DIM), vec(B_QK_DIM),
                  pl.BlockSpec((B_V_DIM, 1), lambda h, i: (0, 0))],
        out_specs=pl.BlockSpec((t, B_V_DIM), lambda h, i: (i, h)),
        out_shape=jax.ShapeDtypeStruct((n_tok, B_V_WIDTH), BF16),
        scratch_shapes=[pltpu.VMEM((2, 1, t), F32), pltpu.VMEM((2, 1, t), F32),
                        pltpu.VMEM((2, B_V_DIM, t), F32), pltpu.VMEM((2, t, t), F32)],
        compiler_params=_cparams("arbitrary", "arbitrary"),
        name="diff_flash_attention",
    )(rel_bias, qt2, kb, vt, *lam_p, subln.reshape(-1, 1))


def _decode_kernel(pt_ref, q_ref, kn_ref, vn_ref, rbrow_ref, lq1_ref, lk1_ref, lq2_ref, lk2_ref, subln_ref,
                   *rest, pages_per_step, page_size, n_new, lam_init):
    P = pages_per_step
    k_refs = rest[0:P]
    v_refs = rest[P:2 * P]
    o_ref = rest[2 * P]
    m_scr, l_scr, acc_scr = rest[2 * P + 1:]
    j = pl.program_id(1)
    last = pl.num_programs(1) - 1
    R = q_ref.shape[0]
    rows_per_map = B_HEADS * n_new
    L = page_size * B_HEADS

    @pl.when(j == 0)
    def _():
        m_scr[...] = jnp.full(m_scr.shape, NEG_INF, F32)
        l_scr[...] = jnp.zeros(l_scr.shape, F32)
        acc_scr[...] = jnp.zeros(acc_scr.shape, F32)

    q = q_ref[...]

    def row_head_query(width):
        r = lax.broadcasted_iota(jnp.int32, (R, width), 0) % rows_per_map
        return r // n_new, r % n_new

    col = lax.broadcasted_iota(jnp.int32, (R, L), 1)
    head, qi = row_head_query(L)
    own_head = (col % B_HEADS) == head

    def page_scores(pi, with_bias):
        k2 = k_refs[pi][...].reshape(L, 2 * B_QK_DIM).astype(BF16)
        s = _dot_nt(q, k2)
        if with_bias:
            dist = qi + page_size - col // B_HEADS
            s = s + _rel_bias_minus_far(dist, lambda b: rbrow_ref[:, b:b + 1])
        return jnp.where(own_head, s, NEG_INF)

    def attend(scores, values):
        m_prev = m_scr[...]
        m_new = m_prev
        for s in scores:
            m_new = jnp.maximum(m_new, jnp.max(s, axis=-1, keepdims=True))
        alpha = jnp.exp(m_prev - m_new)
        ps = [jnp.exp(s - m_new) for s in scores]
        l_new = alpha * l_scr[...]
        acc = alpha * acc_scr[...]
        for p in ps:
            l_new = l_new + jnp.sum(p, axis=-1, keepdims=True)
        for p, v in zip(ps, values):
            acc = acc + _dot(p.astype(BF16), v)
        l_scr[...] = l_new
        acc_scr[...] = acc
        m_scr[...] = m_new

    def page_values():
        return [v_refs[pi][...].reshape(L, B_V_DIM).astype(BF16) for pi in range(P)]

    @pl.when(j != last)
    def _():
        attend([page_scores(pi, False) for pi in range(P)], page_values())

    @pl.when(j == last)
    def _():
        scores = [page_scores(pi, pi == P - 1) for pi in range(P)]
        kn = kn_ref[...]
        s = _dot_nt(q, kn)
        nl = kn.shape[0]
        ncol = lax.broadcasted_iota(jnp.int32, (R, nl), 1)
        nhead, nqi = row_head_query(nl)
        tok = ncol // B_HEADS
        s = s + _rel_bias_minus_far(nqi - tok, lambda b: rbrow_ref[:, b:b + 1])
        s = jnp.where((ncol % B_HEADS == nhead) & (tok <= nqi), s, NEG_INF)
        attend(scores + [s], page_values() + [vn_ref[...]])
        lam = _lambda(lq1_ref, lk1_ref, lq2_ref, lk2_ref, lam_init)
        o = acc_scr[...] / l_scr[...]
        o = o[0:rows_per_map] - lam * o[rows_per_map:2 * rows_per_map]
        o_ref[...] = _sub_norm(o, subln_ref[...], lam_init).astype(o_ref.dtype)


def _decode_attention(page_table, qd, kn, vn, rbrow, lam_p, subln, cache_k, cache_v, layer, lam_init, n_new,
                      pages_per_step):
    nseq, n_pages = page_table.shape
    page_size = cache_k.shape[2]
    P = pages_per_step
    n_steps = n_pages // P
    R = qd.shape[1]
    hd = 2 * B_QK_DIM
    vec = lambda n: pl.BlockSpec((1, n), lambda b, j, pt: (0, 0))

    def page_spec(pi, width):
        return pl.BlockSpec((None, None, page_size, B_HEADS, width),
                            lambda b, j, pt: (layer, pt[b, j * P + pi], 0, 0, 0))

    kern = functools.partial(_decode_kernel, pages_per_step=P, page_size=page_size, n_new=n_new, lam_init=lam_init)
    grid_spec = pltpu.PrefetchScalarGridSpec(
        num_scalar_prefetch=1,
        grid=(nseq, n_steps),
        in_specs=[pl.BlockSpec((None, R, hd), lambda b, j, pt: (b, 0, 0)),
                  pl.BlockSpec((None, kn.shape[1], hd), lambda b, j, pt: (b, 0, 0)),
                  pl.BlockSpec((None, vn.shape[1], B_V_DIM), lambda b, j, pt: (b, 0, 0)),
                  pl.BlockSpec(rbrow.shape, lambda b, j, pt: (0, 0)),
                  vec(B_QK_DIM), vec(B_QK_DIM), vec(B_QK_DIM), vec(B_QK_DIM), vec(B_V_DIM)]
                 + [page_spec(pi, hd) for pi in range(P)] + [page_spec(pi, B_V_DIM) for pi in range(P)],
        out_specs=pl.BlockSpec((None, R // 2, B_V_DIM), lambda b, j, pt: (b, 0, 0)),
        scratch_shapes=[pltpu.VMEM((R, 1), F32), pltpu.VMEM((R, 1), F32), pltpu.VMEM((R, B_V_DIM), F32)],
    )
    return pl.pallas_call(
        kern,
        grid_spec=grid_spec,
        out_shape=jax.ShapeDtypeStruct((nseq, R // 2, B_V_DIM), BF16),
        compiler_params=_cparams("arbitrary", "arbitrary"),
        name="diff_paged_decode",
    )(page_table, qd, kn, vn, rbrow, *lam_p, subln, *([cache_k] * P), *([cache_v] * P))


def _merge_kernel(ya_ref, yb_ref, ga_ref, gb_ref, wa_ref, wb_ref, o_ref):
    ma = _dot(ya_ref[...], wa_ref[...])
    mb = _dot(yb_ref[...], wb_ref[...])
    o_ref[...] = (_sigmoid(ga_ref[...]) * ma + _sigmoid(gb_ref[...]) * mb).astype(o_ref.dtype)


def _merge(ya, yb, gates, wa, wb, tm):
    m = ya.shape[0]
    d = wa.shape[1]
    return pl.pallas_call(
        _merge_kernel,
        grid=(m // tm,),
        in_specs=[pl.BlockSpec((tm, ya.shape[1]), lambda i: (i, 0)),
                  pl.BlockSpec((tm, yb.shape[1]), lambda i: (i, 0)),
                  pl.BlockSpec((tm, d), lambda i: (i, 0)),
                  pl.BlockSpec((tm, d), lambda i: (i, 1)),
                  pl.BlockSpec(wa.shape, lambda i: (0, 0)),
                  pl.BlockSpec(wb.shape, lambda i: (0, 0))],
        out_specs=pl.BlockSpec((tm, d), lambda i: (i, 0)),
        out_shape=jax.ShapeDtypeStruct((m, d), BF16),
        compiler_params=_cparams("arbitrary"),
        name="branch_merge",
    )(ya, yb, gates, gates, wa, wb)


def _out_router_kernel(mg_ref, x_ref, g1_ref, gain_ref, scale_ref, shift_ref, wo_ref, wrh_ref, wrm_ref, br_ref,
                       x1_ref, h2_ref, meta_ref, cnt_ref, carry_scr):
    @pl.when(pl.program_id(0) == 0)
    def _():
        carry_scr[...] = jnp.zeros(carry_scr.shape, F32)

    x1 = x_ref[...] + g1_ref[...] * _dot(mg_ref[...], wo_ref[...])
    x1_ref[...] = x1
    h2 = _modnorm(x1, gain_ref[...], scale_ref[...], shift_ref[...])
    h2_ref[...] = h2
    hi, mid, _ = _split3(h2)
    logit = _dot(hi, wrh_ref[...]) + _dot(hi, wrm_ref[...]) + _dot(mid, wrh_ref[...]) + br_ref[...]
    lane = lax.broadcasted_iota(jnp.int32, logit.shape, 1)
    big = jnp.int32(2 ** 30)

    def first_max(mask):
        v = jnp.max(jnp.where(mask, logit, NEG_INF), axis=-1, keepdims=True)
        idx = jnp.min(jnp.where(mask & (logit == v), lane, big), axis=-1, keepdims=True)
        return v, idx

    coarse = lane < N_GROUPS
    vc, g_idx = first_max(coarse)
    p_g = 1.0 / jnp.sum(jnp.where(coarse, jnp.exp(logit - vc), 0.0), axis=-1, keepdims=True)
    lo = N_GROUPS + g_idx * EXPERTS_PER_GROUP
    fine = (lane >= lo) & (lane < lo + EXPERTS_PER_GROUP)
    v1, i1 = first_max(fine)
    v2, i2 = first_max(fine & (lane != i1))
    e2 = jnp.exp(v2 - v1)
    p1 = 1.0 / (1.0 + e2)
    tm = logit.shape[0]
    member = ((lane == i1) | (lane == i2)).astype(BF16)
    rr = lax.broadcasted_iota(jnp.int32, (tm, tm), 0)
    cc = lax.broadcasted_iota(jnp.int32, (tm, tm), 1)
    rank = carry_scr[...] + _dot((cc < rr).astype(BF16), member)
    carry = carry_scr[...] + jnp.sum(member.astype(F32), axis=0, keepdims=True)
    carry_scr[...] = carry
    cnt_ref[...] = carry
    pick = lambda mask, x: jnp.sum(jnp.where(mask, x, 0.0), axis=-1, keepdims=True)
    fields = [(i1 - N_GROUPS).astype(F32), (i2 - N_GROUPS).astype(F32), pick(lane == i1, rank), pick(lane == i2, rank),
              p_g * p1, p_g * (e2 * p1)]
    meta = jnp.zeros(logit.shape, F32)
    for n, f in enumerate(fields):
        meta = jnp.where(lane == n, f, meta)
    meta_ref[...] = meta


META_E1, META_E2, META_R1, META_R2, META_W1, META_W2 = range(6)


def _out_router(mg, x, g1, gain, scale, shift, wo, wrh, wrm, br, tm):
    m, d = x.shape
    full = lambda a: pl.BlockSpec(a.shape, lambda i: (0, 0))
    row = lambda n: pl.BlockSpec((tm, n), lambda i: (i, 0))
    return pl.pallas_call(
        _out_router_kernel,
        grid=(m // tm,),
        in_specs=[row(d), row(d), _mod_spec(g1, tm), full(gain), _mod_spec(scale, tm), _mod_spec(shift, tm),
                  full(wo), full(wrh), full(wrm), full(br)],
        out_specs=[row(d), row(d), row(ROUTER_LANES), pl.BlockSpec((1, ROUTER_LANES), lambda i: (0, 0))],
        out_shape=[jax.ShapeDtypeStruct((m, d), F32), jax.ShapeDtypeStruct((m, d), F32),
                   jax.ShapeDtypeStruct((m, ROUTER_LANES), F32), jax.ShapeDtypeStruct((1, ROUTER_LANES), F32)],
        scratch_shapes=[pltpu.VMEM((1, ROUTER_LANES), F32)],
        compiler_params=_cparams("arbitrary"),
        name="out_proj_router",
    )(mg, x, g1, gain, scale, shift, wo, wrh, wrm, br)


EXPERT_TILE = 256


def _row_copy(src_ref, src_row, dst_ref, dst_row, sem):
    return pltpu.make_async_copy(src_ref.at[pl.ds(src_row, 1)], dst_ref.at[pl.ds(dst_row, 1)], sem)


def _dispatch_kernel(slots_ref, h_ref, xs_in_ref, xs_ref, sem):
    tb = h_ref.shape[0]
    base = pl.program_id(0) * tb

    def issue(t, c):
        for k in range(2):
            _row_copy(h_ref, t, xs_ref, slots_ref[2 * (base + t) + k], sem).start()
        return c

    def drain(t, c):
        for k in range(2):
            _row_copy(h_ref, 0, xs_ref, 0, sem).wait()
        return c

    lax.fori_loop(0, tb, issue, 0, unroll=8)
    lax.fori_loop(0, tb, drain, 0, unroll=8)


def _moe_dispatch(h2, slots, n_rows, tb):
    m, d = h2.shape
    grid_spec = pltpu.PrefetchScalarGridSpec(
        num_scalar_prefetch=1,
        grid=(m // tb,),
        in_specs=[pl.BlockSpec((tb, d), lambda i, s: (i, 0)), pl.BlockSpec(memory_space=pl.ANY)],
        out_specs=pl.BlockSpec(memory_space=pl.ANY),
        scratch_shapes=[pltpu.SemaphoreType.DMA(())],
    )
    return pl.pallas_call(
        _dispatch_kernel,
        grid_spec=grid_spec,
        out_shape=jax.ShapeDtypeStruct((n_rows, d), F32),
        input_output_aliases={2: 0},
        compiler_params=_cparams("arbitrary"),
        name="moe_dispatch",
    )(slots, h2, jnp.zeros((n_rows, d), F32))


def _expert_kernel(te_ref, nu_ref, x_ref, wg_ref, wu_ref, wd_ref, y_ref):
    t = pl.program_id(0)

    @pl.when(t < nu_ref[0])
    def _():
        x = x_ref[...].astype(BF16)
        hg = _dot(x, wg_ref[...].astype(BF16))
        hu = _dot(x, wu_ref[...].astype(BF16))
        act = _silu(hg) * hu
        y_ref[...] = _dot(act.astype(BF16), wd_ref[...].astype(BF16))

    @pl.when(t >= nu_ref[0])
    def _():
        y_ref[...] = jnp.zeros(y_ref.shape, F32)


def _moe_experts(xs, tile_expert, n_used, wg, wu, wd):
    n_rows, d = xs.shape
    f = wg.shape[2]
    n_tiles = n_rows // EXPERT_TILE
    used = lambda t, nu: jnp.minimum(t, nu[0] - 1)
    grid_spec = pltpu.PrefetchScalarGridSpec(
        num_scalar_prefetch=2,
        grid=(n_tiles,),
        in_specs=[pl.BlockSpec((EXPERT_TILE, d), lambda t, te, nu: (used(t, nu), 0)),
                  pl.BlockSpec((None, d, f), lambda t, te, nu: (te[t], 0, 0)),
                  pl.BlockSpec((None, d, f), lambda t, te, nu: (te[t], 0, 0)),
                  pl.BlockSpec((None, f, d), lambda t, te, nu: (te[t], 0, 0))],
        out_specs=pl.BlockSpec((EXPERT_TILE, d), lambda t, te, nu: (t, 0)),
    )
    return pl.pallas_call(
        _expert_kernel,
        grid_spec=grid_spec,
        out_shape=jax.ShapeDtypeStruct((n_rows, d), F32),
        compiler_params=_cparams("arbitrary"),
        name="moe_experts",
    )(tile_expert, n_used, xs, wg, wu, wd)


def _combine_kernel(slots_ref, ys_ref, meta_ref, x1_ref, g2_ref, o_ref, ybuf, sem):
    tb = x1_ref.shape[0]
    base = pl.program_id(0) * tb

    def issue(t, c):
        for k in range(2):
            _row_copy(ys_ref, slots_ref[2 * (base + t) + k], ybuf.at[k], t, sem).start()
        return c

    def drain(t, c):
        for k in range(2):
            _row_copy(ys_ref, 0, ybuf.at[k], 0, sem).wait()
        return c

    lax.fori_loop(0, tb, issue, 0, unroll=8)
    lax.fori_loop(0, tb, drain, 0, unroll=8)
    meta = meta_ref[...]
    lane = lax.broadcasted_iota(jnp.int32, meta.shape, 1)
    w1 = jnp.sum(jnp.where(lane == META_W1, meta, 0.0), axis=-1, keepdims=True)
    w2 = jnp.sum(jnp.where(lane == META_W2, meta, 0.0), axis=-1, keepdims=True)
    o_ref[...] = x1_ref[...] + g2_ref[...] * (w1 * ybuf[0] + w2 * ybuf[1])


def _moe_combine(ys, slots, meta, x1, g2, tb):
    m, d = x1.shape
    grid_spec = pltpu.PrefetchScalarGridSpec(
        num_scalar_prefetch=1,
        grid=(m // tb,),
        in_specs=[pl.BlockSpec(memory_space=pl.ANY),
                  pl.BlockSpec((tb, ROUTER_LANES), lambda i, s: (i, 0)),
                  pl.BlockSpec((tb, d), lambda i, s: (i, 0)),
                  _mod_spec(g2, tb)],
        out_specs=pl.BlockSpec((tb, d), lambda i, s: (i, 0)),
        scratch_shapes=[pltpu.VMEM((2, tb, d), F32), pltpu.SemaphoreType.DMA(())],
    )
    return pl.pallas_call(
        _combine_kernel,
        grid_spec=grid_spec,
        out_shape=jax.ShapeDtypeStruct((m, d), F32),
        compiler_params=_cparams("arbitrary"),
        name="moe_combine",
    )(slots, ys, meta, x1, g2)


def _moe_sorted(h2, meta, cnt, x1, g2, wg, wu, wd, tb):
    m = h2.shape[0]
    n_tiles = (2 * m) // EXPERT_TILE + N_EXPERTS
    counts = cnt[0, N_GROUPS:N_GROUPS + N_EXPERTS].astype(jnp.int32)
    padded = (counts + EXPERT_TILE - 1) // EXPERT_TILE * EXPERT_TILE
    seg_end = jnp.cumsum(padded)
    seg_start = seg_end - padded
    n_used = (seg_end[-1] // EXPERT_TILE).astype(jnp.int32).reshape(1)
    tile = jnp.minimum(jnp.arange(n_tiles, dtype=jnp.int32), n_used[0] - 1)
    tile_expert = jnp.sum(seg_end[None, :] // EXPERT_TILE <= tile[:, None], axis=1).astype(jnp.int32)
    e12 = meta[:, META_E1:META_E2 + 1].astype(jnp.int32)
    r12 = meta[:, META_R1:META_R2 + 1].astype(jnp.int32)
    slots = (seg_start[e12] + r12).reshape(-1)
    xs = _moe_dispatch(h2, slots, n_tiles * EXPERT_TILE, tb)
    ys = _moe_experts(xs, tile_expert, n_used, wg, wu, wd)
    return _moe_combine(ys, slots, meta, x1, g2, tb)


def _pick(m, prefs):
    for t in prefs:
        if m % t == 0:
            return t
    return m


def _prep_layer_params(l, w_in, mu_shift, w0, w_decay_up, a0, w_iclr_up, w_gate_up, k_k, k_a, r_k, ln_x_w, ln_x_b,
                       q_norm, k_norm, lambda_q1, lambda_k1, lambda_q2, lambda_k2, subln, w_branch_a, w_branch_b,
                       w_out, w_router_coarse, b_router_coarse, w_router_fine, b_router_fine,
                       w_exp_gate, w_exp_up, w_exp_down, norm1, norm2):
    d = w_in.shape[1]
    wi = w_in[l]
    c_qkv = A_COLS + 2 * B_QK_WIDTH + B_V_WIDTH
    w_lora = jnp.zeros((LORA_IN, 3 * A_WIDTH), F32)
    w_lora = w_lora.at[0:DECAY_LORA, 0:A_WIDTH].set(w_decay_up[l])
    w_lora = w_lora.at[DECAY_LORA:DECAY_LORA + ICLR_LORA, A_WIDTH:2 * A_WIDTH].set(w_iclr_up[l])
    w_lora = w_lora.at[DECAY_LORA + ICLR_LORA:, 2 * A_WIDTH:].set(w_gate_up[l])
    lane = jnp.arange(LANES)
    sel64 = (lane[:, None] // A_HEAD_DIM == lane[None, :] // A_HEAD_DIM).astype(BF16)
    w_r = jnp.zeros((d, ROUTER_LANES), F32)
    w_r = w_r.at[:, 0:N_GROUPS].set(w_router_coarse[l])
    w_r = w_r.at[:, N_GROUPS:N_GROUPS + N_EXPERTS].set(
        jnp.transpose(w_router_fine[l], (1, 0, 2)).reshape(d, N_EXPERTS))
    w_rh = w_r.astype(BF16)
    w_rm = (w_r - w_rh.astype(F32)).astype(BF16)
    b_r = jnp.zeros((1, ROUTER_LANES), F32)
    b_r = b_r.at[0, 0:N_GROUPS].set(b_router_coarse[l])
    b_r = b_r.at[0, N_GROUPS:N_GROUPS + N_EXPERTS].set(b_router_fine[l].reshape(-1))
    row = lambda a: a.reshape(1, -1)
    return dict(
        w_in_a=wi[:, 0:A_COLS].astype(BF16), w_in_qkv=wi[:, A_COLS:c_qkv].astype(BF16),
        w_in_g=wi[:, c_qkv:].astype(BF16),
        mu_shift=row(mu_shift[l]), w_lora=w_lora.astype(BF16), w0=row(w0[l]), a0=row(a0[l]), k_k=row(k_k[l]),
        k_a=row(k_a[l]), r_k=row(r_k[l]), ln_x_w=row(ln_x_w[l]), ln_x_b=row(ln_x_b[l]), sel64=sel64,
        q_gain=row(jnp.tile(q_norm[l].reshape(-1), B_HEADS)), k_gain=row(jnp.tile(k_norm[l].reshape(-1), B_HEADS)),
        lam=(row(lambda_q1[l]), row(lambda_k1[l]), row(lambda_q2[l]), row(lambda_k2[l])), subln=row(subln[l]),
        w_a=w_branch_a[l].astype(BF16), w_b=w_branch_b[l].astype(BF16), w_out=w_out[l].astype(BF16),
        w_rh=w_rh, w_rm=w_rm, b_r=b_r,
        w_eg=w_exp_gate[l], w_eu=w_exp_up[l], w_ed=w_exp_down[l],
        norm1=row(norm1[l]), norm2=row(norm2[l]))


def _trunk_rows(x2, mods, p, tm):
    sh1, sc1 = mods[0], mods[1]
    tm_a = _pick(x2.shape[0], (tm,))
    ua, h = _norm_proj(x2, p["norm1"], sc1, sh1, p["w_in_a"], min(tm_a, 512), A_COLS // 2)
    uqkv = _proj(h, p["w_in_qkv"], tm_a, 1024)
    ug = _proj(h, p["w_in_g"], tm_a, 1024)
    return ua, uqkv, ug


def _tail_rows(x2, ya, yb, ug, mods, p, tm):
    _, _, g1, sh2, sc2, g2 = mods
    m = x2.shape[0]
    mg = _merge(ya, yb, ug, p["w_a"], p["w_b"], _pick(m, (tm,)))
    tb = _pick(m, (256, 128))
    x1, h2, meta, cnt = _out_router(mg, x2, g1, p["norm2"], sc2, sh2, p["w_out"], p["w_rh"], p["w_rm"], p["b_r"], tb)
    return _moe_sorted(h2, meta, cnt, x1, g2, p["w_eg"], p["w_eu"], p["w_ed"], tb)


def kernel(x_prompt, x_sample, c_prompt, c_sample, cache_k, cache_v, state_wkv, state_shift, page_table, rel_bias, w_ada, b_ada, norm1, norm2, w_in, mu_shift, w0, w_decay_up, a0, w_iclr_up, w_gate_up, k_k, k_a, r_k, ln_x_w, ln_x_b, q_norm, k_norm, lambda_q1, lambda_k1, lambda_q2, lambda_k2, subln, w_branch_a, w_branch_b, w_out, w_router_coarse, b_router_coarse, w_router_fine, b_router_fine, w_exp_gate, w_exp_up, w_exp_down):
    depth = w_in.shape[0]
    bp, tp, d = x_prompt.shape
    bs, ts, _ = x_sample.shape
    n_new_pad = 8
    kv_new_rows = LANES // B_HEADS
    yp = x_prompt.reshape(bp * tp, d)
    ys = x_sample.reshape(bs * ts, d)
    c_all = jnp.concatenate([c_prompt, c_sample], axis=0)
    c_rows = -(-c_all.shape[0] // 8) * 8
    c_all = jnp.pad(c_all, ((0, c_rows - c_all.shape[0]), (0, 0)))
    outs = [[] for _ in range(8)]
    for l in range(depth):
        lam_init = 0.8 - 0.6 * math.exp(-0.3 * l)
        p = _prep_layer_params(l, w_in, mu_shift, w0, w_decay_up, a0, w_iclr_up, w_gate_up, k_k, k_a, r_k, ln_x_w,
                               ln_x_b, q_norm, k_norm, lambda_q1, lambda_k1, lambda_q2, lambda_k2, subln,
                               w_branch_a, w_branch_b, w_out, w_router_coarse, b_router_coarse, w_router_fine,
                               b_router_fine, w_exp_gate, w_exp_up, w_exp_down, norm1, norm2)
        ada = _ada(c_all, w_ada[l], b_ada[l])
        mods_p = [jnp.repeat(a, tp, axis=0) if bp > 1 else a for a in jnp.split(ada[0:bp], 6, axis=-1)]
        ua, uqkv, ug = _trunk_rows(yp, mods_p, p, _pick(bp * tp, (1024, 512, 256, 128)))
        ua3 = ua.reshape(bp, tp, A_COLS)
        ya, wkv_p = _rwkv(ua3, jnp.zeros((bp, 1, A_COLS), F32),
                          jnp.zeros((bp, A_HEADS, A_HEAD_DIM, A_HEAD_DIM), F32), p, 64, None)
        qt2, kf, kb, vf, vt = _qk_prep(uqkv, p["q_gain"], p["k_gain"], p["sel64"], 512, True)
        t_blk = _pick(tp, (512, 256, 128))
        yb = jnp.concatenate(
            [_flash_prompt(qt2[:, :, b * tp:(b + 1) * tp], kb[b * tp:(b + 1) * tp], vt[:, b * tp:(b + 1) * tp],
                           rel_bias, p["lam"], p["subln"], lam_init, t_blk) for b in range(bp)], axis=0)
        yp = _tail_rows(yp, ya.reshape(bp * tp, A_WIDTH), yb, ug, mods_p, p, 512)
        outs[0].append(kf.reshape(bp, tp, B_HEADS, 2 * B_QK_DIM))
        outs[1].append(vf.reshape(bp, tp, B_HEADS, B_V_DIM))
        outs[2].append(wkv_p)
        outs[3].append(ua3[:, -1])
        mods_s = [jnp.repeat(a, ts, axis=0) for a in jnp.split(ada[bp:bp + bs], 6, axis=-1)]
        ua, uqkv, ug = _trunk_rows(ys, mods_s, p, 128)
        ua3 = ua.reshape(bs, ts, A_COLS)
        ua_pad = jnp.pad(ua3, ((0, 0), (0, n_new_pad - ts), (0, 0)))
        ya, wkv_s = _rwkv(ua_pad, state_shift[l][:, None, :], state_wkv[l], p, n_new_pad, ts)
        ya = ya[:, 0:ts].reshape(bs * ts, A_WIDTH)
        q2, kf, kb, vf, vb = _qk_prep(uqkv, p["q_gain"], p["k_gain"], p["sel64"], _pick(bs * ts, (128,)), False)
        qd = q2.reshape(2, bs, ts, B_HEADS, 2 * B_QK_DIM).transpose(1, 0, 3, 2, 4).reshape(
            bs, 2 * B_HEADS * ts, 2 * B_QK_DIM)
        kn = jnp.pad(kb.reshape(bs, ts * B_HEADS, 2 * B_QK_DIM), ((0, 0), (0, (kv_new_rows - ts) * B_HEADS), (0, 0)))
        vn = jnp.pad(vb.reshape(bs, ts * B_HEADS, B_V_DIM), ((0, 0), (0, (kv_new_rows - ts) * B_HEADS), (0, 0)))
        rbrow = jnp.tile(jnp.repeat(rel_bias.T, ts, axis=0), (2, 1))
        od = _decode_attention(page_table, qd, kn, vn, rbrow, p["lam"], p["subln"], cache_k, cache_v, l, lam_init,
                               ts, _pick(page_table.shape[1], (16, 8, 4, 2, 1)))
        yb = od.reshape(bs, B_HEADS, ts, B_V_DIM).transpose(0, 2, 1, 3).reshape(bs * ts, B_V_WIDTH)
        ys = _tail_rows(ys, ya, yb, ug, mods_s, p, 128)
        outs[4].append(kf.reshape(bs, ts, B_HEADS, 2 * B_QK_DIM))
        outs[5].append(vf.reshape(bs, ts, B_HEADS, B_V_DIM))
        outs[6].append(wkv_s)
        outs[7].append(ua3[:, -1])
    st = [jnp.stack(o) for o in outs]
    return (yp.reshape(bp, tp, d), ys.reshape(bs, ts, d), st[0], st[1], st[2], st[3], st[4], st[5], st[6], st[7])
```

```python
import functools
import math

import jax
import jax.numpy as jnp
from jax import lax
from jax.experimental import pallas as pl
from jax.experimental.pallas import tpu as pltpu

F32 = jnp.float32
BF16 = jnp.bfloat16

A_HEADS = 16
A_HEAD_DIM = 64
A_WIDTH = A_HEADS * A_HEAD_DIM
DECAY_LORA = 64
ICLR_LORA = 64
GATE_LORA = 128
LORA_IN = DECAY_LORA + ICLR_LORA + GATE_LORA
A_COLS = 3 * A_WIDTH + LORA_IN
GN_EPS = 64e-5
B_HEADS = 8
B_QK_DIM = 64
B_V_DIM = 2 * B_QK_DIM
B_QK_WIDTH = B_HEADS * 2 * B_QK_DIM
B_V_WIDTH = B_HEADS * B_V_DIM
NUM_BUCKETS = 32
MAX_DISTANCE = 128
N_GROUPS = 4
EXPERTS_PER_GROUP = 8
N_EXPERTS = N_GROUPS * EXPERTS_PER_GROUP
NORM_EPS = 1e-6
NEG_INF = -1e30

LANES = 128
ROUTER_LANES = LANES
VMEM_LIMIT = 56 * 1024 * 1024


def _cparams(*sem):
    return pltpu.CompilerParams(dimension_semantics=sem, vmem_limit_bytes=VMEM_LIMIT)


def _dot(a, b):
    return jnp.dot(a, b, preferred_element_type=F32)


def _dot_nt(a, b):
    return lax.dot_general(a, b, (((1,), (1,)), ((), ())), preferred_element_type=F32)


def _dot_tn(a, b):
    return lax.dot_general(a, b, (((0,), (0,)), ((), ())), preferred_element_type=F32)


def _split3(x):
    hi = x.astype(BF16)
    r1 = x - hi.astype(F32)
    mid = r1.astype(BF16)
    lo = (r1 - mid.astype(F32)).astype(BF16)
    return hi, mid, lo


def _dot_sel_rhs(x, sel):
    hi, mid, lo = _split3(x)
    return _dot(hi, sel) + _dot(mid, sel) + _dot(lo, sel)


def _dot_sel_lhs(sel, x):
    hi, mid, lo = _split3(x)
    return _dot(sel, hi) + _dot(sel, mid) + _dot(sel, lo)


def _group_sum(x, sel):
    parts = [_dot_sel_rhs(x[:, t * LANES:(t + 1) * LANES], sel) for t in range(x.shape[1] // LANES)]
    return parts[0] if len(parts) == 1 else jnp.concatenate(parts, axis=1)


def _sigmoid(x):
    return 1.0 / (1.0 + jnp.exp(-x))


def _silu(x):
    return x * _sigmoid(x)


def _rel_bias_minus_far(dist, table_fn):
    n = jnp.maximum(dist, 0)
    max_exact = NUM_BUCKETS // 2
    nf = jnp.maximum(n, 1).astype(F32)
    large = max_exact + (jnp.log(nf / max_exact) / math.log(MAX_DISTANCE / max_exact)
                         * (NUM_BUCKETS - max_exact)).astype(jnp.int32)
    large = jnp.minimum(large, NUM_BUCKETS - 1)
    bucket = jnp.where(n < max_exact, n, large)
    far = table_fn(NUM_BUCKETS - 1)
    out = jnp.zeros(dist.shape, F32)
    for b in range(NUM_BUCKETS - 1):
        out = jnp.where(bucket == b, table_fn(b) - far, out)
    return out


def _ada_kernel(c_ref, w_ref, b_ref, o_ref):
    s = _silu(c_ref[...])
    o_ref[...] = _dot(s.astype(BF16), w_ref[...].astype(BF16)) + b_ref[...]


def _ada(c, w, b):
    m, d = c.shape
    n = w.shape[1]
    tn = 1024
    return pl.pallas_call(
        _ada_kernel,
        grid=(n // tn,),
        in_specs=[pl.BlockSpec((m, d), lambda j: (0, 0)),
                  pl.BlockSpec((d, tn), lambda j: (0, j)),
                  pl.BlockSpec((1, tn), lambda j: (0, j))],
        out_specs=pl.BlockSpec((m, tn), lambda j: (0, j)),
        out_shape=jax.ShapeDtypeStruct((m, n), F32),
        compiler_params=_cparams("arbitrary"),
        name="ada_modulation",
    )(c, w, b.reshape(1, n))


def _modnorm(x, gain, scale, shift):
    ms = jnp.mean(x * x, axis=-1, keepdims=True)
    h = x * lax.rsqrt(ms + NORM_EPS) * gain
    return h * (1.0 + scale) + shift


def _norm_proj_kernel(x_ref, gain_ref, scale_ref, shift_ref, w_ref, o_ref, h_ref):
    @pl.when(pl.program_id(1) == 0)
    def _():
        h_ref[...] = _modnorm(x_ref[...], gain_ref[...], scale_ref[...], shift_ref[...]).astype(BF16)

    o_ref[...] = _dot(h_ref[...], w_ref[...])


def _proj_kernel(h_ref, w_ref, o_ref):
    o_ref[...] = _dot(h_ref[...], w_ref[...])


def _proj(h, w, col0, n, tm, tn):
    m, d = h.shape
    return pl.pallas_call(
        _proj_kernel,
        grid=(m // tm, n // tn),
        in_specs=[pl.BlockSpec((tm, d), lambda i, j: (i, 0)),
                  pl.BlockSpec((pl.Element(d), pl.Element(tn)),
                               lambda i, j: (0, pl.multiple_of(col0 + j * tn, LANES)))],
        out_specs=pl.BlockSpec((tm, tn), lambda i, j: (i, j)),
        out_shape=jax.ShapeDtypeStruct((m, n), F32),
        compiler_params=_cparams("arbitrary", "arbitrary"),
        name="in_proj",
    )(h, w)


def _mod_spec(mod, tm):
    if mod.shape[0] == 1:
        return pl.BlockSpec((1, mod.shape[1]), lambda i, *_: (0, 0))
    return pl.BlockSpec((tm, mod.shape[1]), lambda i, *_: (i, 0))


def _norm_proj(x, gain, scale, shift, w, n, tm, tn):
    m, d = x.shape
    return pl.pallas_call(
        _norm_proj_kernel,
        grid=(m // tm, n // tn),
        in_specs=[pl.BlockSpec((tm, d), lambda i, j: (i, 0)),
                  pl.BlockSpec((1, d), lambda i, j: (0, 0)),
                  _mod_spec(scale, tm), _mod_spec(shift, tm),
                  pl.BlockSpec((d, tn), lambda i, j: (0, j))],
        out_specs=[pl.BlockSpec((tm, tn), lambda i, j: (i, j)), pl.BlockSpec((tm, d), lambda i, j: (i, 0))],
        out_shape=[jax.ShapeDtypeStruct((m, n), F32), jax.ShapeDtypeStruct((m, d), BF16)],
        compiler_params=_cparams("arbitrary", "arbitrary"),
        name="norm_in_proj",
    )(x, gain, scale, shift, w)


def _rwkv_kernel(ua_ref, shift0_ref, s0_ref, mu_ref, wl_ref, w0_ref, a0_ref, kk_ref, ka_ref, rk_ref,
                 lnw_ref, lnb_ref, sel_ref, y_ref, sout_ref, state_scr, prev_scr, yh_scr, *, chunk, n_valid):
    c = pl.program_id(1)
    n_chunks = pl.num_programs(1)
    C = chunk
    W = A_WIDTH

    @pl.when(c == 0)
    def _():
        state_scr[...] = s0_ref[...]
        prev_scr[...] = shift0_ref[...]

    ua = ua_ref[...]
    row = lax.broadcasted_iota(jnp.int32, (C, 1), 0)
    prev = jnp.where(row == 0, prev_scr[...], pltpu.roll(ua, 1, axis=0))
    prev_scr[...] = ua[C - 1:C, :]
    us = ua + (prev - ua) * mu_ref[...]
    r = us[:, 0:W]
    kr = us[:, W:2 * W]
    vr = us[:, 2 * W:3 * W]
    lin = us[:, 3 * W:3 * W + LORA_IN]
    lane = lax.broadcasted_iota(jnp.int32, (1, LORA_IN), 1)
    act = jnp.where(lane < DECAY_LORA, jnp.tanh(lin),
                    jnp.where(lane < DECAY_LORA + ICLR_LORA, lin, _sigmoid(lin)))
    lora = _dot(act.astype(BF16), wl_ref[...])
    z = -(w0_ref[...] + lora[:, 0:W])
    softplus = jnp.maximum(z, 0.0) + jnp.log(1.0 + jnp.exp(-jnp.abs(z)))
    ld = -jnp.exp(-softplus - 0.5)
    iclr = _sigmoid(a0_ref[...] + lora[:, W:2 * W])
    gate_o = lora[:, 2 * W:3 * W]

    sel = sel_ref[...]
    kk = kr * kk_ref[...]
    kk = kk / jnp.maximum(jnp.sqrt(_group_sum(kk * kk, sel)), 1e-12)
    k_mod = kr * (1.0 + (iclr - 1.0) * ka_ref[...])
    if n_valid is not None:
        valid = (c * C + row) < n_valid
        ld = jnp.where(valid, ld, 0.0)
        kk = jnp.where(valid, kk, 0.0)
        k_mod = jnp.where(valid, k_mod, 0.0)
        vr = jnp.where(valid, vr, 0.0)

    ri = lax.broadcasted_iota(jnp.int32, (C, C), 0)
    ci = lax.broadcasted_iota(jnp.int32, (C, C), 1)
    cum = _dot_sel_lhs((ci <= ri).astype(BF16), ld)
    g_inv = jnp.exp(-cum)
    g_last = jnp.exp(cum[C - 1:C, :])
    at = -kk * jnp.exp(cum - ld)
    bt = kk * iclr * g_inv
    kt = k_mod * g_inv
    bl = bt * g_last
    kl = kt * g_last
    rt = r * jnp.exp(cum)
    eye = (ci == ri).astype(F32)
    n_double = max(int(math.ceil(math.log2(C))) - 1, 0)
    r2 = lax.broadcasted_iota(jnp.int32, (2 * C, 2 * C), 0)
    c2 = lax.broadcasted_iota(jnp.int32, (2 * C, 2 * C), 1)
    mask2 = ((r2 < C) & ((c2 % C) < r2)) | ((r2 >= C) & ((c2 % C) <= (r2 - C)))
    zeros_c = jnp.zeros((C, A_HEAD_DIM), F32)
    heads = range(A_HEADS)

    def hsl(x, h):
        return x[:, h * A_HEAD_DIM:(h + 1) * A_HEAD_DIM]

    def stack(top, bottom):
        return jnp.concatenate([top, bottom], axis=0).astype(BF16)

    atb = [hsl(at, h).astype(BF16) for h in heads]
    ar = [stack(hsl(at, h), hsl(rt, h)) for h in heads]
    bk = [stack(hsl(bt, h), hsl(kt, h)) for h in heads]
    zv = [stack(zeros_c, hsl(vr, h)) for h in heads]
    bkl = [stack(hsl(bl, h), hsl(kl, h)) for h in heads]
    m2 = [jnp.where(mask2, _dot_nt(ar[h], bk[h]), 0.0) for h in heads]
    m2b = [m.astype(BF16) for m in m2]
    bvk = [_dot(m2b[h], zv[h]) for h in heads]
    p = [m[0:C, 0:C] for m in m2]
    t_inv = [eye + a for a in p]
    for _ in range(n_double):
        pb = [x.astype(BF16) for x in p]
        p = [_dot(x, x) for x in pb]
        t_inv = [t + _dot(t.astype(BF16), x.astype(BF16)) for t, x in zip(t_inv, p)]
    tb = [t.astype(BF16) for t in t_inv]
    w = [_dot(tb[h], atb[h]) for h in heads]
    z = [_dot(tb[h], bvk[h][0:C].astype(BF16)) for h in heads]
    s0 = [state_scr[h] for h in heads]
    ws = [_dot_nt(stack(w[h], hsl(rt, h)), s0[h].astype(BF16)) for h in heads]
    u = [ws[h][0:C] + z[h] for h in heads]
    y = [ws[h][C:2 * C] + bvk[h][C:2 * C] + _dot(m2[h][C:2 * C, 0:C].astype(BF16), u[h].astype(BF16))
         for h in heads]
    s_new = [s0[h] * hsl(g_last, h) + _dot_tn(stack(u[h], hsl(vr, h)), bkl[h]) for h in heads]
    for h in heads:
        yh_scr[:, h * A_HEAD_DIM:(h + 1) * A_HEAD_DIM] = y[h]
        state_scr[h] = s_new[h]

    y = yh_scr[...]
    inv_n = 1.0 / A_HEAD_DIM
    mu_y = _group_sum(y, sel) * inv_n
    dy = y - mu_y
    var = _group_sum(dy * dy, sel) * inv_n
    yn = dy * lax.rsqrt(var + GN_EPS) * lnw_ref[...] + lnb_ref[...]
    bonus = _group_sum(r * k_mod * rk_ref[...], sel) * vr
    y_ref[...] = ((yn + bonus) * gate_o).astype(y_ref.dtype)

    @pl.when(c == n_chunks - 1)
    def _():
        sout_ref[...] = state_scr[...]


def _rwkv(ua, shift0, s0, p, chunk, n_valid):
    b, t, _ = ua.shape
    n_chunks = t // chunk
    vec = lambda n: pl.BlockSpec((1, n), lambda i, j: (0, 0))
    kern = functools.partial(_rwkv_kernel, chunk=chunk, n_valid=n_valid)
    return pl.pallas_call(
        kern,
        grid=(b, n_chunks),
        in_specs=[pl.BlockSpec((None, chunk, A_COLS), lambda i, j: (i, j, 0)),
                  pl.BlockSpec((None, 1, A_COLS), lambda i, j: (i, 0, 0)),
                  pl.BlockSpec((None, A_HEADS, A_HEAD_DIM, A_HEAD_DIM), lambda i, j: (i, 0, 0, 0)),
                  vec(A_COLS),
                  pl.BlockSpec((LORA_IN, 3 * A_WIDTH), lambda i, j: (0, 0)),
                  vec(A_WIDTH), vec(A_WIDTH), vec(A_WIDTH), vec(A_WIDTH), vec(A_WIDTH), vec(A_WIDTH), vec(A_WIDTH),
                  pl.BlockSpec((LANES, LANES), lambda i, j: (0, 0))],
        out_specs=[pl.BlockSpec((None, chunk, A_WIDTH), lambda i, j: (i, j, 0)),
                   pl.BlockSpec((None, A_HEADS, A_HEAD_DIM, A_HEAD_DIM), lambda i, j: (i, 0, 0, 0))],
        out_shape=[jax.ShapeDtypeStruct((b, t, A_WIDTH), BF16),
                   jax.ShapeDtypeStruct((b, A_HEADS, A_HEAD_DIM, A_HEAD_DIM), F32)],
        scratch_shapes=[pltpu.VMEM((A_HEADS, A_HEAD_DIM, A_HEAD_DIM), F32),
                        pltpu.VMEM((1, A_COLS), F32),
                        pltpu.VMEM((chunk, A_WIDTH), F32)],
        compiler_params=_cparams("arbitrary", "arbitrary"),
        name="rwkv7_chunked",
    )(ua, shift0, s0, p["mu_shift"], p["w_lora"], p["w0"], p["a0"], p["k_k"], p["k_a"], p["r_k"],
      p["ln_x_w"], p["ln_x_b"], p["sel64"])


def _qk_prep_kernel(u_ref, qg_ref, kg_ref, sel_ref, q_ref, kf_ref, kb_ref, vf_ref, vb_ref, *, transposed):
    sel = sel_ref[...]
    wq = B_QK_WIDTH
    uq = u_ref[:, 0:wq]
    uk = u_ref[:, wq:2 * wq]
    uv = u_ref[:, 2 * wq:2 * wq + B_V_WIDTH]
    inv_n = 1.0 / B_QK_DIM
    q = uq * lax.rsqrt(_group_sum(uq * uq, sel) * inv_n + NORM_EPS) * qg_ref[...]
    k = uk * lax.rsqrt(_group_sum(uk * uk, sel) * inv_n + NORM_EPS) * kg_ref[...]
    q = q * (B_QK_DIM ** -0.5)
    kf_ref[...] = k
    kb_ref[...] = k.astype(BF16)
    vf_ref[...] = uv
    if transposed:
        qt = q.T
        ch = lax.broadcasted_iota(jnp.int32, (wq, 1), 0)
        first_map = (ch % (2 * B_QK_DIM)) < B_QK_DIM
        q_ref[0] = jnp.where(first_map, qt, 0.0).astype(BF16)
        q_ref[1] = jnp.where(first_map, 0.0, qt).astype(BF16)
        vb_ref[...] = uv.T.astype(BF16)
    else:
        ch = lax.broadcasted_iota(jnp.int32, (1, wq), 1)
        first_map = (ch % (2 * B_QK_DIM)) < B_QK_DIM
        q_ref[0] = jnp.where(first_map, q, 0.0).astype(BF16)
        q_ref[1] = jnp.where(first_map, 0.0, q).astype(BF16)
        vb_ref[...] = uv.astype(BF16)


def _qk_prep(u_qkv, qg, kg, sel, tm, transposed):
    m = u_qkv.shape[0]
    w = B_QK_WIDTH
    row = lambda n: pl.BlockSpec((tm, n), lambda i: (i, 0))
    if transposed:
        q_spec, q_shape = pl.BlockSpec((2, w, tm), lambda i: (0, 0, i)), (2, w, m)
        v_spec, v_shape = pl.BlockSpec((w, tm), lambda i: (0, i)), (w, m)
    else:
        q_spec, q_shape = pl.BlockSpec((2, tm, w), lambda i: (0, i, 0)), (2, m, w)
        v_spec, v_shape = row(w), (m, w)
    return pl.pallas_call(
        functools.partial(_qk_prep_kernel, transposed=transposed),
        grid=(m // tm,),
        in_specs=[row(3 * w),
                  pl.BlockSpec((1, w), lambda i: (0, 0)), pl.BlockSpec((1, w), lambda i: (0, 0)),
                  pl.BlockSpec((LANES, LANES), lambda i: (0, 0))],
        out_specs=[q_spec, row(w), row(w), row(w), v_spec],
        out_shape=[jax.ShapeDtypeStruct(q_shape, BF16), jax.ShapeDtypeStruct((m, w), F32),
                   jax.ShapeDtypeStruct((m, w), BF16), jax.ShapeDtypeStruct((m, w), F32),
                   jax.ShapeDtypeStruct(v_shape, BF16)],
        compiler_params=_cparams("arbitrary"),
        name="qk_norm",
    )(u_qkv, qg, kg, sel)


def _lambda(lq1_ref, lk1_ref, lq2_ref, lk2_ref, lam_init):
    s1 = jnp.sum(lq1_ref[...] * lk1_ref[...], axis=-1, keepdims=True)
    s2 = jnp.sum(lq2_ref[...] * lk2_ref[...], axis=-1, keepdims=True)
    return jnp.exp(s1) - jnp.exp(s2) + lam_init


def _sub_norm(o, subln, lam_init):
    ms = jnp.mean(o * o, axis=-1, keepdims=True)
    return o * lax.rsqrt(ms + NORM_EPS) * subln * (1.0 - lam_init)


def _flash_kernel(rb_ref, q_ref, k_ref, v_ref, lq1_ref, lk1_ref, lq2_ref, lk2_ref, subln_ref, o_ref,
                  m_scr, l_scr, acc_scr, bias_scr, *, t, lam_init):
    h = pl.program_id(0)
    i = pl.program_id(1)

    @pl.when(i == 0)
    def _():
        kk = lax.broadcasted_iota(jnp.int32, (t, t), 0)
        qq = lax.broadcasted_iota(jnp.int32, (t, t), 1)
        table = lambda b: rb_ref[b, h]
        bias_scr[0] = jnp.where(qq >= kk, _rel_bias_minus_far(qq - kk, table), NEG_INF)
        bias_scr[1] = _rel_bias_minus_far(qq - kk + t, table)

    m_scr[...] = jnp.full(m_scr.shape, NEG_INF, F32)
    l_scr[...] = jnp.zeros(l_scr.shape, F32)
    acc_scr[...] = jnp.zeros(acc_scr.shape, F32)

    def step(j, bias):
        start = pl.multiple_of(j * t, t)
        k = k_ref[pl.ds(start, t), :]
        sc = [_dot(k, q_ref[mp]) for mp in range(2)]
        ps = []
        for mp in range(2):
            x = sc[mp] if bias is None else sc[mp] + bias
            m_prev = m_scr[mp]
            m_new = jnp.maximum(m_prev, jnp.max(x, axis=0, keepdims=True))
            alpha = jnp.exp(m_prev - m_new)
            p = jnp.exp(x - m_new)
            l_scr[mp] = alpha * l_scr[mp] + jnp.sum(p, axis=0, keepdims=True)
            acc_scr[mp] = alpha * acc_scr[mp]
            m_scr[mp] = m_new
            ps.append(p.astype(BF16))
        v = v_ref[:, pl.ds(start, t)]
        for mp in range(2):
            acc_scr[mp] += _dot(v, ps[mp])

    def far_block(j, carry):
        step(j, None)
        return carry

    lax.fori_loop(0, i - 1, far_block, 0)

    @pl.when(i > 0)
    def _():
        step(i - 1, bias_scr[1])

    step(i, bias_scr[0])
    lam = _lambda(lq1_ref, lk1_ref, lq2_ref, lk2_ref, lam_init)
    o = acc_scr[0] / l_scr[0] - lam * (acc_scr[1] / l_scr[1])
    ms = jnp.mean(o * o, axis=0, keepdims=True)
    o = o * lax.rsqrt(ms + NORM_EPS) * subln_ref[...] * (1.0 - lam_init)
    o_ref[...] = o.T.astype(o_ref.dtype)


def _flash_prompt(qt2, kb, vt, rel_bias, lam_p, subln, lam_init, t):
    n_tok = kb.shape[0]
    nb = n_tok // t
    hd = 2 * B_QK_DIM
    vec = lambda n: pl.BlockSpec((1, n), lambda h, i: (0, 0))
    kern = functools.partial(_flash_kernel, t=t, lam_init=lam_init)
    return pl.pallas_call(
        kern,
        grid=(B_HEADS, nb),
        in_specs=[pl.BlockSpec(memory_space=pltpu.SMEM),
                  pl.BlockSpec((2, hd, t), lambda h, i: (0, h, i)),
                  pl.BlockSpec((n_tok, hd), lambda h, i: (0, h)),
                  pl.BlockSpec((B_V_DIM, n_tok), lambda h, i: (h, 0)),
                  vec(B_QK_DIM), vec(B_QK_DIM), vec(B_QK_DIM), vec(B_QK_DIM),
                  pl.BlockSpec((B_V_DIM, 1), lambda h, i: (0, 0))],
        out_specs=pl.BlockSpec((t, B_V_DIM), lambda h, i: (i, h)),
        out_shape=jax.ShapeDtypeStruct((n_tok, B_V_WIDTH), BF16),
        scratch_shapes=[pltpu.VMEM((2, 1, t), F32), pltpu.VMEM((2, 1, t), F32),
                        pltpu.VMEM((2, B_V_DIM, t), F32), pltpu.VMEM((2, t, t), F32)],
        compiler_params=_cparams("arbitrary", "arbitrary"),
        name="diff_flash_attention",
    )(rel_bias, qt2, kb, vt, *lam_p, subln.reshape(-1, 1))


def _decode_kernel(pt_ref, q_ref, kn_ref, vn_ref, rbrow_ref, lq1_ref, lk1_ref, lq2_ref, lk2_ref, subln_ref,
                   *rest, pages_per_step, page_size, n_new, lam_init):
    P = pages_per_step
    k_refs = rest[0:P]
    v_refs = rest[P:2 * P]
    o_ref = rest[2 * P]
    m_scr, l_scr, acc_scr = rest[2 * P + 1:]
    j = pl.program_id(1)
    last = pl.num_programs(1) - 1
    R = q_ref.shape[0]
    rows_per_map = B_HEADS * n_new
    L = page_size * B_HEADS

    @pl.when(j == 0)
    def _():
        m_scr[...] = jnp.full(m_scr.shape, NEG_INF, F32)
        l_scr[...] = jnp.zeros(l_scr.shape, F32)
        acc_scr[...] = jnp.zeros(acc_scr.shape, F32)

    q = q_ref[...]

    def row_head_query(width):
        r = lax.broadcasted_iota(jnp.int32, (R, width), 0) % rows_per_map
        return r // n_new, r % n_new

    col = lax.broadcasted_iota(jnp.int32, (R, L), 1)
    head, qi = row_head_query(L)
    own_head = (col % B_HEADS) == head

    def page_scores(pi, with_bias):
        k2 = k_refs[pi][...].reshape(L, 2 * B_QK_DIM).astype(BF16)
        s = _dot_nt(q, k2)
        if with_bias:
            dist = qi + page_size - col // B_HEADS
            s = s + _rel_bias_minus_far(dist, lambda b: rbrow_ref[:, b:b + 1])
        return jnp.where(own_head, s, NEG_INF)

    def attend(scores, values):
        m_prev = m_scr[...]
        m_new = m_prev
        for s in scores:
            m_new = jnp.maximum(m_new, jnp.max(s, axis=-1, keepdims=True))
        alpha = jnp.exp(m_prev - m_new)
        ps = [jnp.exp(s - m_new) for s in scores]
        l_new = alpha * l_scr[...]
        acc = alpha * acc_scr[...]
        for p in ps:
            l_new = l_new + jnp.sum(p, axis=-1, keepdims=True)
        for p, v in zip(ps, values):
            acc = acc + _dot(p.astype(BF16), v)
        l_scr[...] = l_new
        acc_scr[...] = acc
        m_scr[...] = m_new

    def page_values():
        return [v_refs[pi][...].reshape(L, B_V_DIM).astype(BF16) for pi in range(P)]

    @pl.when(j != last)
    def _():
        attend([page_scores(pi, False) for pi in range(P)], page_values())

    @pl.when(j == last)
    def _():
        scores = [page_scores(pi, pi == P - 1) for pi in range(P)]
        kn = kn_ref[...]
        s = _dot_nt(q, kn)
        nl = kn.shape[0]
        ncol = lax.broadcasted_iota(jnp.int32, (R, nl), 1)
        nhead, nqi = row_head_query(nl)
        tok = ncol // B_HEADS
        s = s + _rel_bias_minus_far(nqi - tok, lambda b: rbrow_ref[:, b:b + 1])
        s = jnp.where((ncol % B_HEADS == nhead) & (tok <= nqi), s, NEG_INF)
        attend(scores + [s], page_values() + [vn_ref[...]])
        lam = _lambda(lq1_ref, lk1_ref, lq2_ref, lk2_ref, lam_init)
        o = acc_scr[...] / l_scr[...]
        o = o[0:rows_per_map] - lam * o[rows_per_map:2 * rows_per_map]
        o_ref[...] = _sub_norm(o, subln_ref[...], lam_init).astype(o_ref.dtype)


def _decode_attention(page_table, qd, kn, vn, rbrow, lam_p, subln, cache_k, cache_v, layer, lam_init, n_new,
                      pages_per_step):
    nseq, n_pages = page_table.shape
    page_size = cache_k.shape[2]
    P = pages_per_step
    n_steps = n_pages // P
    R = qd.shape[1]
    hd = 2 * B_QK_DIM
    vec = lambda n: pl.BlockSpec((1, n), lambda b, j, pt: (0, 0))

    def page_spec(pi, width):
        return pl.BlockSpec((None, None, page_size, B_HEADS, width),
                            lambda b, j, pt: (layer, pt[b, j * P + pi], 0, 0, 0))

    kern = functools.partial(_decode_kernel, pages_per_step=P, page_size=page_size, n_new=n_new, lam_init=lam_init)
    grid_spec = pltpu.PrefetchScalarGridSpec(
        num_scalar_prefetch=1,
        grid=(nseq, n_steps),
        in_specs=[pl.BlockSpec((None, R, hd), lambda b, j, pt: (b, 0, 0)),
                  pl.BlockSpec((None, kn.shape[1], hd), lambda b, j, pt: (b, 0, 0)),
                  pl.BlockSpec((None, vn.shape[1], B_V_DIM), lambda b, j, pt: (b, 0, 0)),
                  pl.BlockSpec(rbrow.shape, lambda b, j, pt: (0, 0)),
                  vec(B_QK_DIM), vec(B_QK_DIM), vec(B_QK_DIM), vec(B_QK_DIM), vec(B_V_DIM)]
                 + [page_spec(pi, hd) for pi in range(P)] + [page_spec(pi, B_V_DIM) for pi in range(P)],
        out_specs=pl.BlockSpec((None, R // 2, B_V_DIM), lambda b, j, pt: (b, 0, 0)),
        scratch_shapes=[pltpu.VMEM((R, 1), F32), pltpu.VMEM((R, 1), F32), pltpu.VMEM((R, B_V_DIM), F32)],
    )
    return pl.pallas_call(
        kern,
        grid_spec=grid_spec,
        out_shape=jax.ShapeDtypeStruct((nseq, R // 2, B_V_DIM), BF16),
        compiler_params=_cparams("arbitrary", "arbitrary"),
        name="diff_paged_decode",
    )(page_table, qd, kn, vn, rbrow, *lam_p, subln, *([cache_k] * P), *([cache_v] * P))


def _merge_kernel(ya_ref, yb_ref, ga_ref, gb_ref, wa_ref, wb_ref, o_ref):
    ma = _dot(ya_ref[...], wa_ref[...])
    mb = _dot(yb_ref[...], wb_ref[...])
    o_ref[...] = (_sigmoid(ga_ref[...]) * ma + _sigmoid(gb_ref[...]) * mb).astype(o_ref.dtype)


def _merge(ya, yb, gates, wa, wb, tm):
    m = ya.shape[0]
    d = wa.shape[1]
    return pl.pallas_call(
        _merge_kernel,
        grid=(m // tm,),
        in_specs=[pl.BlockSpec((tm, ya.shape[1]), lambda i: (i, 0)),
                  pl.BlockSpec((tm, yb.shape[1]), lambda i: (i, 0)),
                  pl.BlockSpec((tm, d), lambda i: (i, 0)),
                  pl.BlockSpec((tm, d), lambda i: (i, 1)),
                  pl.BlockSpec(wa.shape, lambda i: (0, 0)),
                  pl.BlockSpec(wb.shape, lambda i: (0, 0))],
        out_specs=pl.BlockSpec((tm, d), lambda i: (i, 0)),
        out_shape=jax.ShapeDtypeStruct((m, d), BF16),
        compiler_params=_cparams("arbitrary"),
        name="branch_merge",
    )(ya, yb, gates, gates, wa, wb)


def _out_router_kernel(mg_ref, x_ref, g1_ref, gain_ref, scale_ref, shift_ref, wo_ref, wrh_ref, wrm_ref, br_ref,
                       cnt0_ref, x1_ref, h2_ref, meta_ref, cnt_ref, carry_scr):
    @pl.when(pl.program_id(0) == 0)
    def _():
        carry_scr[...] = cnt0_ref[...]

    x1 = x_ref[...] + g1_ref[...] * _dot(mg_ref[...], wo_ref[...])
    x1_ref[...] = x1
    h2 = _modnorm(x1, gain_ref[...], scale_ref[...], shift_ref[...])
    h2_ref[...] = h2
    hi, mid, _ = _split3(h2)
    logit = _dot(hi, wrh_ref[...]) + _dot(hi, wrm_ref[...]) + _dot(mid, wrh_ref[...]) + br_ref[...]
    lane = lax.broadcasted_iota(jnp.int32, logit.shape, 1)
    big = jnp.int32(2 ** 30)

    def first_max(mask):
        v = jnp.max(jnp.where(mask, logit, NEG_INF), axis=-1, keepdims=True)
        idx = jnp.min(jnp.where(mask & (logit == v), lane, big), axis=-1, keepdims=True)
        return v, idx

    coarse = lane < N_GROUPS
    vc, g_idx = first_max(coarse)
    p_g = 1.0 / jnp.sum(jnp.where(coarse, jnp.exp(logit - vc), 0.0), axis=-1, keepdims=True)
    lo = N_GROUPS + g_idx * EXPERTS_PER_GROUP
    fine = (lane >= lo) & (lane < lo + EXPERTS_PER_GROUP)
    v1, i1 = first_max(fine)
    v2, i2 = first_max(fine & (lane != i1))
    e2 = jnp.exp(v2 - v1)
    p1 = 1.0 / (1.0 + e2)
    tm = logit.shape[0]
    member = ((lane == i1) | (lane == i2)).astype(BF16)
    rr = lax.broadcasted_iota(jnp.int32, (tm, tm), 0)
    cc = lax.broadcasted_iota(jnp.int32, (tm, tm), 1)
    rank = carry_scr[...] + _dot((cc < rr).astype(BF16), member)
    carry = carry_scr[...] + jnp.sum(member.astype(F32), axis=0, keepdims=True)
    carry_scr[...] = carry
    cnt_ref[...] = carry
    pick = lambda mask, x: jnp.sum(jnp.where(mask, x, 0.0), axis=-1, keepdims=True)
    fields = [(i1 - N_GROUPS).astype(F32), (i2 - N_GROUPS).astype(F32), pick(lane == i1, rank), pick(lane == i2, rank),
              p_g * p1, p_g * (e2 * p1)]
    meta = jnp.zeros(logit.shape, F32)
    for n, f in enumerate(fields):
        meta = jnp.where(lane == n, f, meta)
    meta_ref[...] = meta


META_E1, META_E2, META_R1, META_R2, META_W1, META_W2 = range(6)


def _out_router(mg, x, g1, gain, scale, shift, wo, wrh, wrm, br, cnt0, tm):
    m, d = x.shape
    full = lambda a: pl.BlockSpec(a.shape, lambda i: (0, 0))
    row = lambda n: pl.BlockSpec((tm, n), lambda i: (i, 0))
    return pl.pallas_call(
        _out_router_kernel,
        grid=(m // tm,),
        in_specs=[row(d), row(d), _mod_spec(g1, tm), full(gain), _mod_spec(scale, tm), _mod_spec(shift, tm),
                  full(wo), full(wrh), full(wrm), full(br), full(cnt0)],
        out_specs=[row(d), row(d), row(ROUTER_LANES), pl.BlockSpec((1, ROUTER_LANES), lambda i: (0, 0))],
        out_shape=[jax.ShapeDtypeStruct((m, d), F32), jax.ShapeDtypeStruct((m, d), F32),
                   jax.ShapeDtypeStruct((m, ROUTER_LANES), F32), jax.ShapeDtypeStruct((1, ROUTER_LANES), F32)],
        scratch_shapes=[pltpu.VMEM((1, ROUTER_LANES), F32)],
        compiler_params=_cparams("arbitrary"),
        name="out_proj_router",
    )(mg, x, g1, gain, scale, shift, wo, wrh, wrm, br, cnt0)


EXPERT_TILE = 256


def _row_copy(src_ref, src_row, dst_ref, dst_row, sem):
    return pltpu.make_async_copy(src_ref.at[pl.ds(src_row, 1)], dst_ref.at[pl.ds(dst_row, 1)], sem)


def _dispatch_kernel(slots_ref, pad_ref, h_ref, *rest, first):
    if first:
        xs_ref, sem, zero_buf = rest
    else:
        _, xs_ref, sem = rest
    tb = h_ref.shape[0]
    base = pl.program_id(0) * tb

    if first:
        @pl.when(pl.program_id(0) == 0)
        def _():
            zero_buf[...] = jnp.zeros(zero_buf.shape, F32)

            def zero_copy(e):
                row = pl.multiple_of(jnp.maximum(pad_ref[e], 0), EXPERT_TILE)
                return pltpu.make_async_copy(zero_buf, xs_ref.at[pl.ds(row, EXPERT_TILE)], sem)

            def start(e, c):
                @pl.when(pad_ref[e] >= 0)
                def _():
                    zero_copy(e).start()
                return c

            def wait(e, c):
                @pl.when(pad_ref[e] >= 0)
                def _():
                    zero_copy(e).wait()
                return c

            lax.fori_loop(0, pad_ref.shape[0], start, 0)
            lax.fori_loop(0, pad_ref.shape[0], wait, 0)

    def issue(t, c):
        for k in range(2):
            _row_copy(h_ref, t, xs_ref, slots_ref[2 * (base + t) + k], sem).start()
        return c

    def drain(t, c):
        for k in range(2):
            _row_copy(h_ref, 0, xs_ref, 0, sem).wait()
        return c

    lax.fori_loop(0, tb, issue, 0, unroll=8)
    lax.fori_loop(0, tb, drain, 0, unroll=8)


def _moe_dispatch(h2, slots, pad_rows, xs, n_rows, tb):
    m, d = h2.shape
    first = xs is None
    grid_spec = pltpu.PrefetchScalarGridSpec(
        num_scalar_prefetch=2,
        grid=(m // tb,),
        in_specs=[pl.BlockSpec((tb, d), lambda i, s, z: (i, 0))] + ([] if first else [pl.BlockSpec(memory_space=pl.ANY)]),
        out_specs=pl.BlockSpec(memory_space=pl.ANY),
        scratch_shapes=[pltpu.SemaphoreType.DMA(())] + ([pltpu.VMEM((EXPERT_TILE, d), F32)] if first else []),
    )
    return pl.pallas_call(
        functools.partial(_dispatch_kernel, first=first),
        grid_spec=grid_spec,
        out_shape=jax.ShapeDtypeStruct((n_rows, d), F32),
        input_output_aliases={} if first else {3: 0},
        compiler_params=_cparams("arbitrary"),
        name="moe_dispatch",
    )(slots, pad_rows, h2, *([] if first else [xs]))


def _expert_kernel(te_ref, nu_ref, x_ref, wg_ref, wu_ref, wd_ref, y_ref):
    t = pl.program_id(0)

    @pl.when(t < nu_ref[0])
    def _():
        x = x_ref[...].astype(BF16)
        hg = _dot(x, wg_ref[...].astype(BF16))
        hu = _dot(x, wu_ref[...].astype(BF16))
        act = _silu(hg) * hu
        y_ref[...] = _dot(act.astype(BF16), wd_ref[...].astype(BF16))

    @pl.when(t >= nu_ref[0])
    def _():
        y_ref[...] = jnp.zeros(y_ref.shape, F32)


def _moe_experts(xs, tile_expert, n_used, wg, wu, wd):
    n_rows, d = xs.shape
    f = wg.shape[2]
    n_tiles = n_rows // EXPERT_TILE
    used = lambda t, nu: jnp.minimum(t, nu[0] - 1)
    grid_spec = pltpu.PrefetchScalarGridSpec(
        num_scalar_prefetch=2,
        grid=(n_tiles,),
        in_specs=[pl.BlockSpec((EXPERT_TILE, d), lambda t, te, nu: (used(t, nu), 0)),
                  pl.BlockSpec((None, d, f), lambda t, te, nu: (te[t], 0, 0)),
                  pl.BlockSpec((None, d, f), lambda t, te, nu: (te[t], 0, 0)),
                  pl.BlockSpec((None, f, d), lambda t, te, nu: (te[t], 0, 0))],
        out_specs=pl.BlockSpec((EXPERT_TILE, d), lambda t, te, nu: (t, 0)),
    )
    return pl.pallas_call(
        _expert_kernel,
        grid_spec=grid_spec,
        out_shape=jax.ShapeDtypeStruct((n_rows, d), F32),
        compiler_params=_cparams("arbitrary"),
        name="moe_experts",
    )(tile_expert, n_used, xs, wg, wu, wd)


def _combine_kernel(slots_ref, ys_ref, meta_ref, x1_ref, g2_ref, o_ref, ybuf, sem):
    tb = x1_ref.shape[0]
    i = pl.program_id(0)

    def issue_block(blk, slot):
        def issue(t, c):
            for k in range(2):
                _row_copy(ys_ref, slots_ref[2 * (blk * tb + t) + k], ybuf.at[slot, k], t, sem.at[slot]).start()
            return c

        lax.fori_loop(0, tb, issue, 0, unroll=8)

    @pl.when(i == 0)
    def _():
        issue_block(0, 0)

    @pl.when(i + 1 < pl.num_programs(0))
    def _():
        issue_block(i + 1, lax.rem(i + 1, 2))

    slot = lax.rem(i, 2)

    def drain(t, c):
        for k in range(2):
            _row_copy(ys_ref, 0, ybuf.at[slot, k], 0, sem.at[slot]).wait()
        return c

    lax.fori_loop(0, tb, drain, 0, unroll=8)
    meta = meta_ref[...]
    lane = lax.broadcasted_iota(jnp.int32, meta.shape, 1)
    w1 = jnp.sum(jnp.where(lane == META_W1, meta, 0.0), axis=-1, keepdims=True)
    w2 = jnp.sum(jnp.where(lane == META_W2, meta, 0.0), axis=-1, keepdims=True)
    y = ybuf[slot]
    o_ref[...] = x1_ref[...] + g2_ref[...] * (w1 * y[0] + w2 * y[1])


def _moe_combine(ys, slots, meta, x1, g2, tb):
    m, d = x1.shape
    grid_spec = pltpu.PrefetchScalarGridSpec(
        num_scalar_prefetch=1,
        grid=(m // tb,),
        in_specs=[pl.BlockSpec(memory_space=pl.ANY),
                  pl.BlockSpec((tb, ROUTER_LANES), lambda i, s: (i, 0)),
                  pl.BlockSpec((tb, d), lambda i, s: (i, 0)),
                  _mod_spec(g2, tb)],
        out_specs=pl.BlockSpec((tb, d), lambda i, s: (i, 0)),
        scratch_shapes=[pltpu.VMEM((2, 2, tb, d), F32), pltpu.SemaphoreType.DMA((2,))],
    )
    return pl.pallas_call(
        _combine_kernel,
        grid_spec=grid_spec,
        out_shape=jax.ShapeDtypeStruct((m, d), F32),
        compiler_params=_cparams("arbitrary"),
        name="moe_combine",
    )(slots, ys, meta, x1, g2)


def _moe_sorted(groups, cnt, wg, wu, wd):
    m = sum(g["h2"].shape[0] for g in groups)
    n_tiles = -(-(2 * m) // EXPERT_TILE) + N_EXPERTS
    counts = cnt[0, N_GROUPS:N_GROUPS + N_EXPERTS].astype(jnp.int32)
    padded = (counts + EXPERT_TILE - 1) // EXPERT_TILE * EXPERT_TILE
    seg_end = jnp.cumsum(padded)
    seg_start = seg_end - padded
    n_used = (seg_end[-1] // EXPERT_TILE).astype(jnp.int32).reshape(1)
    tile = jnp.minimum(jnp.arange(n_tiles, dtype=jnp.int32), n_used[0] - 1)
    tile_expert = jnp.sum(seg_end[None, :] // EXPERT_TILE <= tile[:, None], axis=1).astype(jnp.int32)
    all_tiles = jnp.arange(n_tiles, dtype=jnp.int32)
    pad_rows = jnp.concatenate([jnp.where(padded > 0, seg_end - EXPERT_TILE, -1),
                                jnp.where(all_tiles >= n_used[0], all_tiles * EXPERT_TILE, -1)]).astype(jnp.int32)
    xs = None
    for g in groups:
        e12 = g["meta"][:, META_E1:META_E2 + 1].astype(jnp.int32)
        r12 = g["meta"][:, META_R1:META_R2 + 1].astype(jnp.int32)
        g["slots"] = (seg_start[e12] + r12).reshape(-1)
        xs = _moe_dispatch(g["h2"], g["slots"], pad_rows, xs, n_tiles * EXPERT_TILE, g["tb"])
    ys = _moe_experts(xs, tile_expert, n_used, wg, wu, wd)
    return [_moe_combine(ys, g["slots"], g["meta"], g["x1"], g["g2"], g["tb"]) for g in groups]


def _pick(m, prefs):
    for t in prefs:
        if m % t == 0:
            return t
    return m


def _prep_layer_params(l, w_in, mu_shift, w0, w_decay_up, a0, w_iclr_up, w_gate_up, k_k, k_a, r_k, ln_x_w, ln_x_b,
                       q_norm, k_norm, lambda_q1, lambda_k1, lambda_q2, lambda_k2, subln, w_branch_a, w_branch_b,
                       w_out, w_router_coarse, b_router_coarse, w_router_fine, b_router_fine,
                       w_exp_gate, w_exp_up, w_exp_down, norm1, norm2):
    d = w_in.shape[1]
    wi = w_in[l]
    c_qkv = A_COLS + 2 * B_QK_WIDTH + B_V_WIDTH
    w_lora = jnp.zeros((LORA_IN, 3 * A_WIDTH), F32)
    w_lora = w_lora.at[0:DECAY_LORA, 0:A_WIDTH].set(w_decay_up[l])
    w_lora = w_lora.at[DECAY_LORA:DECAY_LORA + ICLR_LORA, A_WIDTH:2 * A_WIDTH].set(w_iclr_up[l])
    w_lora = w_lora.at[DECAY_LORA + ICLR_LORA:, 2 * A_WIDTH:].set(w_gate_up[l])
    lane = jnp.arange(LANES)
    sel64 = (lane[:, None] // A_HEAD_DIM == lane[None, :] // A_HEAD_DIM).astype(BF16)
    w_r = jnp.zeros((d, ROUTER_LANES), F32)
    w_r = w_r.at[:, 0:N_GROUPS].set(w_router_coarse[l])
    w_r = w_r.at[:, N_GROUPS:N_GROUPS + N_EXPERTS].set(
        jnp.transpose(w_router_fine[l], (1, 0, 2)).reshape(d, N_EXPERTS))
    w_rh = w_r.astype(BF16)
    w_rm = (w_r - w_rh.astype(F32)).astype(BF16)
    b_r = jnp.zeros((1, ROUTER_LANES), F32)
    b_r = b_r.at[0, 0:N_GROUPS].set(b_router_coarse[l])
    b_r = b_r.at[0, N_GROUPS:N_GROUPS + N_EXPERTS].set(b_router_fine[l].reshape(-1))
    row = lambda a: a.reshape(1, -1)
    return dict(
        w_in=wi.astype(BF16),
        mu_shift=row(mu_shift[l]), w_lora=w_lora.astype(BF16), w0=row(w0[l]), a0=row(a0[l]), k_k=row(k_k[l]),
        k_a=row(k_a[l]), r_k=row(r_k[l]), ln_x_w=row(ln_x_w[l]), ln_x_b=row(ln_x_b[l]), sel64=sel64,
        q_gain=row(jnp.tile(q_norm[l].reshape(-1), B_HEADS)), k_gain=row(jnp.tile(k_norm[l].reshape(-1), B_HEADS)),
        lam=(row(lambda_q1[l]), row(lambda_k1[l]), row(lambda_q2[l]), row(lambda_k2[l])), subln=row(subln[l]),
        w_a=w_branch_a[l].astype(BF16), w_b=w_branch_b[l].astype(BF16), w_out=w_out[l].astype(BF16),
        w_rh=w_rh, w_rm=w_rm, b_r=b_r,
        w_eg=w_exp_gate[l], w_eu=w_exp_up[l], w_ed=w_exp_down[l],
        norm1=row(norm1[l]), norm2=row(norm2[l]))


def _trunk_rows(x2, mods, p, tm):
    sh1, sc1 = mods[0], mods[1]
    tm_a = _pick(x2.shape[0], (tm,))
    c_qkv = 2 * B_QK_WIDTH + B_V_WIDTH
    ua, h = _norm_proj(x2, p["norm1"], sc1, sh1, p["w_in"], A_COLS, min(tm_a, 512), A_COLS // 2)
    uqkv = _proj(h, p["w_in"], A_COLS, c_qkv, tm_a, 1024)
    ug = _proj(h, p["w_in"], A_COLS + c_qkv, p["w_in"].shape[1] - A_COLS - c_qkv, tm_a, 1024)
    return ua, uqkv, ug


def _tail_rows(x2, ya, yb, ug, mods, p, tm, cnt0):
    _, _, g1, sh2, sc2, g2 = mods
    m = x2.shape[0]
    mg = _merge(ya, yb, ug, p["w_a"], p["w_b"], _pick(m, (tm,)))
    tb = _pick(m, (256, 128))
    x1, h2, meta, cnt = _out_router(mg, x2, g1, p["norm2"], sc2, sh2, p["w_out"], p["w_rh"], p["w_rm"], p["b_r"],
                                    cnt0, tb)
    return dict(h2=h2, meta=meta, x1=x1, g2=g2, tb=tb), cnt


def kernel(x_prompt, x_sample, c_prompt, c_sample, cache_k, cache_v, state_wkv, state_shift, page_table, rel_bias, w_ada, b_ada, norm1, norm2, w_in, mu_shift, w0, w_decay_up, a0, w_iclr_up, w_gate_up, k_k, k_a, r_k, ln_x_w, ln_x_b, q_norm, k_norm, lambda_q1, lambda_k1, lambda_q2, lambda_k2, subln, w_branch_a, w_branch_b, w_out, w_router_coarse, b_router_coarse, w_router_fine, b_router_fine, w_exp_gate, w_exp_up, w_exp_down):
    depth = w_in.shape[0]
    bp, tp, d = x_prompt.shape
    bs, ts, _ = x_sample.shape
    n_new_pad = 8
    kv_new_rows = LANES // B_HEADS
    yp = x_prompt.reshape(bp * tp, d)
    ys = x_sample.reshape(bs * ts, d)
    c_all = jnp.concatenate([c_prompt, c_sample], axis=0)
    c_rows = -(-c_all.shape[0] // 8) * 8
    c_all = jnp.pad(c_all, ((0, c_rows - c_all.shape[0]), (0, 0)))
    outs = [[] for _ in range(8)]
    for l in range(depth):
        lam_init = 0.8 - 0.6 * math.exp(-0.3 * l)
        p = _prep_layer_params(l, w_in, mu_shift, w0, w_decay_up, a0, w_iclr_up, w_gate_up, k_k, k_a, r_k, ln_x_w,
                               ln_x_b, q_norm, k_norm, lambda_q1, lambda_k1, lambda_q2, lambda_k2, subln,
                               w_branch_a, w_branch_b, w_out, w_router_coarse, b_router_coarse, w_router_fine,
                               b_router_fine, w_exp_gate, w_exp_up, w_exp_down, norm1, norm2)
        ada = _ada(c_all, w_ada[l], b_ada[l])
        mods_p = [jnp.repeat(a, tp, axis=0) if bp > 1 else a for a in jnp.split(ada[0:bp], 6, axis=-1)]
        ua, uqkv, ug = _trunk_rows(yp, mods_p, p, _pick(bp * tp, (1024, 512, 256, 128)))
        ua3 = ua.reshape(bp, tp, A_COLS)
        ya, wkv_p = _rwkv(ua3, jnp.zeros((bp, 1, A_COLS), F32),
                          jnp.zeros((bp, A_HEADS, A_HEAD_DIM, A_HEAD_DIM), F32), p, 64, None)
        qt2, kf, kb, vf, vt = _qk_prep(uqkv, p["q_gain"], p["k_gain"], p["sel64"], 512, True)
        t_blk = _pick(tp, (512, 256, 128))
        yb = jnp.concatenate(
            [_flash_prompt(qt2[:, :, b * tp:(b + 1) * tp], kb[b * tp:(b + 1) * tp], vt[:, b * tp:(b + 1) * tp],
                           rel_bias, p["lam"], p["subln"], lam_init, t_blk) for b in range(bp)], axis=0)
        grp_p, cnt_p = _tail_rows(yp, ya.reshape(bp * tp, A_WIDTH), yb, ug, mods_p, p, 512,
                                  jnp.zeros((1, ROUTER_LANES), F32))
        outs[0].append(kf.reshape(bp, tp, B_HEADS, 2 * B_QK_DIM))
        outs[1].append(vf.reshape(bp, tp, B_HEADS, B_V_DIM))
        outs[2].append(wkv_p)
        outs[3].append(ua3[:, -1])
        mods_s = [jnp.repeat(a, ts, axis=0) for a in jnp.split(ada[bp:bp + bs], 6, axis=-1)]
        ua, uqkv, ug = _trunk_rows(ys, mods_s, p, 128)
        ua3 = ua.reshape(bs, ts, A_COLS)
        ua_pad = jnp.pad(ua3, ((0, 0), (0, n_new_pad - ts), (0, 0)))
        ya, wkv_s = _rwkv(ua_pad, state_shift[l][:, None, :], state_wkv[l], p, n_new_pad, ts)
        ya = ya[:, 0:ts].reshape(bs * ts, A_WIDTH)
        q2, kf, kb, vf, vb = _qk_prep(uqkv, p["q_gain"], p["k_gain"], p["sel64"], _pick(bs * ts, (128,)), False)
        qd = q2.reshape(2, bs, ts, B_HEADS, 2 * B_QK_DIM).transpose(1, 0, 3, 2, 4).reshape(
            bs, 2 * B_HEADS * ts, 2 * B_QK_DIM)
        kn = jnp.pad(kb.reshape(bs, ts * B_HEADS, 2 * B_QK_DIM), ((0, 0), (0, (kv_new_rows - ts) * B_HEADS), (0, 0)))
        vn = jnp.pad(vb.reshape(bs, ts * B_HEADS, B_V_DIM), ((0, 0), (0, (kv_new_rows - ts) * B_HEADS), (0, 0)))
        rbrow = jnp.tile(jnp.repeat(rel_bias.T, ts, axis=0), (2, 1))
        od = _decode_attention(page_table, qd, kn, vn, rbrow, p["lam"], p["subln"], cache_k, cache_v, l, lam_init,
                               ts, _pick(page_table.shape[1], (16, 8, 4, 2, 1)))
        yb = od.reshape(bs, B_HEADS, ts, B_V_DIM).transpose(0, 2, 1, 3).reshape(bs * ts, B_V_WIDTH)
        grp_s, cnt_all = _tail_rows(ys, ya, yb, ug, mods_s, p, 128, cnt_p)
        yp, ys = _moe_sorted([grp_p, grp_s], cnt_all, p["w_eg"], p["w_eu"], p["w_ed"])
        outs[4].append(kf.reshape(bs, ts, B_HEADS, 2 * B_QK_DIM))
        outs[5].append(vf.reshape(bs, ts, B_HEADS, B_V_DIM))
        outs[6].append(wkv_s)
        outs[7].append(ua3[:, -1])
    st = [jnp.stack(o) for o in outs]
    return (yp.reshape(bp, tp, d), ys.reshape(bs, ts, d), st[0], st[1], st[2], st[3], st[4], st[5], st[6], st[7])
```

```python
import functools
import math

import jax
import jax.numpy as jnp
from jax import lax
from jax.experimental import pallas as pl
from jax.experimental.pallas import tpu as pltpu

F32 = jnp.float32
BF16 = jnp.bfloat16

A_HEADS = 16
A_HEAD_DIM = 64
A_WIDTH = A_HEADS * A_HEAD_DIM
DECAY_LORA = 64
ICLR_LORA = 64
GATE_LORA = 128
LORA_IN = DECAY_LORA + ICLR_LORA + GATE_LORA
A_COLS = 3 * A_WIDTH + LORA_IN
GN_EPS = 64e-5
B_HEADS = 8
B_QK_DIM = 64
B_V_DIM = 2 * B_QK_DIM
B_QK_WIDTH = B_HEADS * 2 * B_QK_DIM
B_V_WIDTH = B_HEADS * B_V_DIM
NUM_BUCKETS = 32
MAX_DISTANCE = 128
N_GROUPS = 4
EXPERTS_PER_GROUP = 8
N_EXPERTS = N_GROUPS * EXPERTS_PER_GROUP
NORM_EPS = 1e-6
NEG_INF = -1e30

LANES = 128
ROUTER_LANES = LANES
VMEM_LIMIT = 56 * 1024 * 1024


def _cparams(*sem):
    return pltpu.CompilerParams(dimension_semantics=sem, vmem_limit_bytes=VMEM_LIMIT)


def _dot(a, b):
    return jnp.dot(a, b, preferred_element_type=F32)


def _dot_nt(a, b):
    return lax.dot_general(a, b, (((1,), (1,)), ((), ())), preferred_element_type=F32)


def _dot_tn(a, b):
    return lax.dot_general(a, b, (((0,), (0,)), ((), ())), preferred_element_type=F32)


def _split3(x):
    hi = x.astype(BF16)
    r1 = x - hi.astype(F32)
    mid = r1.astype(BF16)
    lo = (r1 - mid.astype(F32)).astype(BF16)
    return hi, mid, lo


def _dot_sel_rhs(x, sel):
    hi, mid, lo = _split3(x)
    return _dot(hi, sel) + _dot(mid, sel) + _dot(lo, sel)


def _dot_sel_lhs(sel, x):
    hi, mid, lo = _split3(x)
    return _dot(sel, hi) + _dot(sel, mid) + _dot(sel, lo)


def _group_sum(x, sel):
    hi = x.astype(BF16)
    mid = (x - hi.astype(F32)).astype(BF16)
    parts = [_dot(hi[:, t * LANES:(t + 1) * LANES], sel) + _dot(mid[:, t * LANES:(t + 1) * LANES], sel)
             for t in range(x.shape[1] // LANES)]
    return parts[0] if len(parts) == 1 else jnp.concatenate(parts, axis=1)


def _sigmoid(x):
    return 1.0 / (1.0 + jnp.exp(-x))


def _silu(x):
    return x * _sigmoid(x)


def _rel_bias_minus_far(dist, table_fn):
    n = jnp.maximum(dist, 0)
    max_exact = NUM_BUCKETS // 2
    nf = jnp.maximum(n, 1).astype(F32)
    large = max_exact + (jnp.log(nf / max_exact) / math.log(MAX_DISTANCE / max_exact)
                         * (NUM_BUCKETS - max_exact)).astype(jnp.int32)
    large = jnp.minimum(large, NUM_BUCKETS - 1)
    bucket = jnp.where(n < max_exact, n, large)
    far = table_fn(NUM_BUCKETS - 1)
    out = jnp.zeros(dist.shape, F32)
    for b in range(NUM_BUCKETS - 1):
        out = jnp.where(bucket == b, table_fn(b) - far, out)
    return out


def _ada_kernel(c_ref, w_ref, b_ref, o_ref):
    s = _silu(c_ref[...])
    o_ref[...] = _dot(s.astype(BF16), w_ref[...].astype(BF16)) + b_ref[...]


def _ada(c, w, b):
    m, d = c.shape
    n = w.shape[1]
    tn = 1024
    return pl.pallas_call(
        _ada_kernel,
        grid=(n // tn,),
        in_specs=[pl.BlockSpec((m, d), lambda j: (0, 0)),
                  pl.BlockSpec((d, tn), lambda j: (0, j)),
                  pl.BlockSpec((1, tn), lambda j: (0, j))],
        out_specs=pl.BlockSpec((m, tn), lambda j: (0, j)),
        out_shape=jax.ShapeDtypeStruct((m, n), F32),
        compiler_params=_cparams("arbitrary"),
        name="ada_modulation",
    )(c, w, b.reshape(1, n))


def _modnorm(x, gain, scale, shift):
    ms = jnp.mean(x * x, axis=-1, keepdims=True)
    h = x * lax.rsqrt(ms + NORM_EPS) * gain
    return h * (1.0 + scale) + shift


def _modnorm_kernel(x_ref, gain_ref, scale_ref, shift_ref, h_ref):
    h_ref[...] = _modnorm(x_ref[...], gain_ref[...], scale_ref[...], shift_ref[...]).astype(BF16)


def _proj_kernel(h_ref, w_ref, o_ref):
    o_ref[...] = _dot(h_ref[...], w_ref[...])


def _proj(h, w, col0, n, tm, tn):
    m, d = h.shape
    return pl.pallas_call(
        _proj_kernel,
        grid=(m // tm, n // tn),
        in_specs=[pl.BlockSpec((tm, d), lambda i, j: (i, 0)),
                  pl.BlockSpec((pl.Element(d), pl.Element(tn)),
                               lambda i, j: (0, pl.multiple_of(col0 + j * tn, LANES)))],
        out_specs=pl.BlockSpec((tm, tn), lambda i, j: (i, j)),
        out_shape=jax.ShapeDtypeStruct((m, n), F32),
        compiler_params=_cparams("arbitrary", "arbitrary"),
        name="in_proj",
    )(h, w)


def _mod_spec(mod, tm):
    if mod.shape[0] == 1:
        return pl.BlockSpec((1, mod.shape[1]), lambda i, *_: (0, 0))
    return pl.BlockSpec((tm, mod.shape[1]), lambda i, *_: (i, 0))


def _modnorm_rows(x, gain, scale, shift, tm):
    m, d = x.shape
    return pl.pallas_call(
        _modnorm_kernel,
        grid=(m // tm,),
        in_specs=[pl.BlockSpec((tm, d), lambda i: (i, 0)),
                  pl.BlockSpec((1, d), lambda i: (0, 0)),
                  _mod_spec(scale, tm), _mod_spec(shift, tm)],
        out_specs=pl.BlockSpec((tm, d), lambda i: (i, 0)),
        out_shape=jax.ShapeDtypeStruct((m, d), BF16),
        compiler_params=_cparams("arbitrary"),
        name="modulated_norm",
    )(x, gain, scale, shift)


def _rwkv_kernel(ua_ref, shift0_ref, s0_ref, mu_ref, wl_ref, w0_ref, a0_ref, kk_ref, ka_ref, rk_ref,
                 lnw_ref, lnb_ref, sel_ref, y_ref, sout_ref, state_scr, prev_scr, yh_scr, *, chunk, n_sub, n_valid):
    c = pl.program_id(1)
    n_chunks = pl.num_programs(1)
    C = chunk
    TB = chunk * n_sub
    W = A_WIDTH

    @pl.when(c == 0)
    def _():
        state_scr[...] = s0_ref[...]
        prev_scr[...] = shift0_ref[...]

    ua = ua_ref[...]
    row = lax.broadcasted_iota(jnp.int32, (TB, 1), 0)
    prev = jnp.where(row == 0, prev_scr[...], pltpu.roll(ua, 1, axis=0))
    prev_scr[...] = ua[TB - 1:TB, :]
    us = ua + (prev - ua) * mu_ref[...]
    r = us[:, 0:W]
    kr = us[:, W:2 * W]
    vr = us[:, 2 * W:3 * W]
    lin = us[:, 3 * W:3 * W + LORA_IN]
    lane = lax.broadcasted_iota(jnp.int32, (1, LORA_IN), 1)
    act = jnp.where(lane < DECAY_LORA, jnp.tanh(lin),
                    jnp.where(lane < DECAY_LORA + ICLR_LORA, lin, _sigmoid(lin)))
    lora = _dot(act.astype(BF16), wl_ref[...])
    z = -(w0_ref[...] + lora[:, 0:W])
    softplus = jnp.maximum(z, 0.0) + jnp.log(1.0 + jnp.exp(-jnp.abs(z)))
    ld = -jnp.exp(-softplus - 0.5)
    iclr = _sigmoid(a0_ref[...] + lora[:, W:2 * W])
    gate_o = lora[:, 2 * W:3 * W]

    sel = sel_ref[...]
    kk = kr * kk_ref[...]
    kk = kk / jnp.maximum(jnp.sqrt(_group_sum(kk * kk, sel)), 1e-12)
    k_mod = kr * (1.0 + (iclr - 1.0) * ka_ref[...])
    if n_valid is not None:
        valid = (c * TB + row) < n_valid
        ld = jnp.where(valid, ld, 0.0)
        kk = jnp.where(valid, kk, 0.0)
        k_mod = jnp.where(valid, k_mod, 0.0)
        vr = jnp.where(valid, vr, 0.0)

    ri = lax.broadcasted_iota(jnp.int32, (TB, TB), 0)
    ci = lax.broadcasted_iota(jnp.int32, (TB, TB), 1)
    same_chunk_prefix = (ci <= ri) & ((ci // C) == (ri // C))
    cum = _dot_sel_lhs(same_chunk_prefix.astype(BF16), ld)
    g_inv = jnp.exp(-cum)
    g_last = [jnp.exp(cum[(s + 1) * C - 1:(s + 1) * C, :]) for s in range(n_sub)]
    g_last_rows = jnp.concatenate([jnp.broadcast_to(g, (C, W)) for g in g_last], axis=0)
    at = -kk * jnp.exp(cum - ld)
    bt = kk * iclr * g_inv
    kt = k_mod * g_inv
    bl = bt * g_last_rows
    kl = kt * g_last_rows
    rt = r * jnp.exp(cum)
    rc = lax.broadcasted_iota(jnp.int32, (C, C), 0)
    cc = lax.broadcasted_iota(jnp.int32, (C, C), 1)
    eye = (cc == rc).astype(F32)
    n_double = max(int(math.ceil(math.log2(C))) - 1, 0)
    r2 = lax.broadcasted_iota(jnp.int32, (2 * C, 2 * C), 0)
    c2 = lax.broadcasted_iota(jnp.int32, (2 * C, 2 * C), 1)
    mask2 = ((r2 < C) & ((c2 % C) < r2)) | ((r2 >= C) & ((c2 % C) <= (r2 - C)))
    zeros_c = jnp.zeros((C, A_HEAD_DIM), F32)
    heads = range(A_HEADS)
    items = [(s, h) for s in range(n_sub) for h in heads]

    def hsl(x, it):
        s, h = it
        return x[s * C:(s + 1) * C, h * A_HEAD_DIM:(h + 1) * A_HEAD_DIM]

    def stack(top, bottom):
        return jnp.concatenate([top, bottom], axis=0).astype(BF16)

    atb = {it: hsl(at, it).astype(BF16) for it in items}
    ar = {it: stack(hsl(at, it), hsl(rt, it)) for it in items}
    bk = {it: stack(hsl(bt, it), hsl(kt, it)) for it in items}
    zv = {it: stack(zeros_c, hsl(vr, it)) for it in items}
    bkl = {it: stack(hsl(bl, it), hsl(kl, it)) for it in items}
    m2 = {it: jnp.where(mask2, _dot_nt(ar[it], bk[it]), 0.0) for it in items}
    bvk = {it: _dot(m2[it].astype(BF16), zv[it]) for it in items}
    p = {it: m2[it][0:C, 0:C] for it in items}
    t_inv = {it: eye + p[it] for it in items}
    for _ in range(n_double):
        pb = {it: p[it].astype(BF16) for it in items}
        p = {it: _dot(pb[it], pb[it]) for it in items}
        t_inv = {it: t_inv[it] + _dot(t_inv[it].astype(BF16), p[it].astype(BF16)) for it in items}
    tb = {it: t_inv[it].astype(BF16) for it in items}
    w = {it: _dot(tb[it], atb[it]) for it in items}
    z = {it: _dot(tb[it], bvk[it][0:C].astype(BF16)) for it in items}
    state = [state_scr[h] for h in heads]
    for s in range(n_sub):
        its = [(s, h) for h in heads]
        ws = [_dot_nt(stack(w[it], hsl(rt, it)), state[it[1]].astype(BF16)) for it in its]
        u = [ws[h][0:C] + z[(s, h)] for h in heads]
        y = [ws[h][C:2 * C] + bvk[(s, h)][C:2 * C]
             + _dot(m2[(s, h)][C:2 * C, 0:C].astype(BF16), u[h].astype(BF16)) for h in heads]
        state = [state[h] * g_last[s][:, h * A_HEAD_DIM:(h + 1) * A_HEAD_DIM]
                 + _dot_tn(stack(u[h], hsl(vr, (s, h))), bkl[(s, h)]) for h in heads]
        for h in heads:
            yh_scr[s * C:(s + 1) * C, h * A_HEAD_DIM:(h + 1) * A_HEAD_DIM] = y[h]
    for h in heads:
        state_scr[h] = state[h]

    y = yh_scr[...]
    inv_n = 1.0 / A_HEAD_DIM
    mu_y = _group_sum(y, sel) * inv_n
    dy = y - mu_y
    var = _group_sum(dy * dy, sel) * inv_n
    yn = dy * lax.rsqrt(var + GN_EPS) * lnw_ref[...] + lnb_ref[...]
    bonus = _group_sum(r * k_mod * rk_ref[...], sel) * vr
    y_ref[...] = ((yn + bonus) * gate_o).astype(y_ref.dtype)

    @pl.when(c == n_chunks - 1)
    def _():
        sout_ref[...] = state_scr[...]


def _rwkv(ua, shift0, s0, p, chunk, n_sub, n_valid):
    b, t, _ = ua.shape
    rows = chunk * n_sub
    n_chunks = t // rows
    vec = lambda n: pl.BlockSpec((1, n), lambda i, j: (0, 0))
    kern = functools.partial(_rwkv_kernel, chunk=chunk, n_sub=n_sub, n_valid=n_valid)
    return pl.pallas_call(
        kern,
        grid=(b, n_chunks),
        in_specs=[pl.BlockSpec((None, rows, A_COLS), lambda i, j: (i, j, 0)),
                  pl.BlockSpec((None, 1, A_COLS), lambda i, j: (i, 0, 0)),
                  pl.BlockSpec((None, A_HEADS, A_HEAD_DIM, A_HEAD_DIM), lambda i, j: (i, 0, 0, 0)),
                  vec(A_COLS),
                  pl.BlockSpec((LORA_IN, 3 * A_WIDTH), lambda i, j: (0, 0)),
                  vec(A_WIDTH), vec(A_WIDTH), vec(A_WIDTH), vec(A_WIDTH), vec(A_WIDTH), vec(A_WIDTH), vec(A_WIDTH),
                  pl.BlockSpec((LANES, LANES), lambda i, j: (0, 0))],
        out_specs=[pl.BlockSpec((None, rows, A_WIDTH), lambda i, j: (i, j, 0)),
                   pl.BlockSpec((None, A_HEADS, A_HEAD_DIM, A_HEAD_DIM), lambda i, j: (i, 0, 0, 0))],
        out_shape=[jax.ShapeDtypeStruct((b, t, A_WIDTH), BF16),
                   jax.ShapeDtypeStruct((b, A_HEADS, A_HEAD_DIM, A_HEAD_DIM), F32)],
        scratch_shapes=[pltpu.VMEM((A_HEADS, A_HEAD_DIM, A_HEAD_DIM), F32),
                        pltpu.VMEM((1, A_COLS), F32),
                        pltpu.VMEM((rows, A_WIDTH), F32)],
        compiler_params=_cparams("arbitrary", "arbitrary"),
        name="rwkv7_chunked",
    )(ua, shift0, s0, p["mu_shift"], p["w_lora"], p["w0"], p["a0"], p["k_k"], p["k_a"], p["r_k"],
      p["ln_x_w"], p["ln_x_b"], p["sel64"])


def _qk_prep_kernel(u_ref, qg_ref, kg_ref, sel_ref, q_ref, kf_ref, kb_ref, vf_ref, vb_ref, *, transposed):
    sel = sel_ref[...]
    wq = B_QK_WIDTH
    uq = u_ref[:, 0:wq]
    uk = u_ref[:, wq:2 * wq]
    uv = u_ref[:, 2 * wq:2 * wq + B_V_WIDTH]
    inv_n = 1.0 / B_QK_DIM
    q = uq * lax.rsqrt(_group_sum(uq * uq, sel) * inv_n + NORM_EPS) * qg_ref[...]
    k = uk * lax.rsqrt(_group_sum(uk * uk, sel) * inv_n + NORM_EPS) * kg_ref[...]
    q = q * (B_QK_DIM ** -0.5)
    kf_ref[...] = k
    kb_ref[...] = k.astype(BF16)
    vf_ref[...] = uv
    if transposed:
        qt = q.T
        ch = lax.broadcasted_iota(jnp.int32, (wq, 1), 0)
        first_map = (ch % (2 * B_QK_DIM)) < B_QK_DIM
        q_ref[0] = jnp.where(first_map, qt, 0.0).astype(BF16)
        q_ref[1] = jnp.where(first_map, 0.0, qt).astype(BF16)
        vb_ref[...] = uv.T.astype(BF16)
    else:
        ch = lax.broadcasted_iota(jnp.int32, (1, wq), 1)
        first_map = (ch % (2 * B_QK_DIM)) < B_QK_DIM
        q_ref[0] = jnp.where(first_map, q, 0.0).astype(BF16)
        q_ref[1] = jnp.where(first_map, 0.0, q).astype(BF16)
        vb_ref[...] = uv.astype(BF16)


def _qk_prep(u_qkv, qg, kg, sel, tm, transposed):
    m = u_qkv.shape[0]
    w = B_QK_WIDTH
    row = lambda n: pl.BlockSpec((tm, n), lambda i: (i, 0))
    if transposed:
        q_spec, q_shape = pl.BlockSpec((2, w, tm), lambda i: (0, 0, i)), (2, w, m)
        v_spec, v_shape = pl.BlockSpec((w, tm), lambda i: (0, i)), (w, m)
    else:
        q_spec, q_shape = pl.BlockSpec((2, tm, w), lambda i: (0, i, 0)), (2, m, w)
        v_spec, v_shape = row(w), (m, w)
    return pl.pallas_call(
        functools.partial(_qk_prep_kernel, transposed=transposed),
        grid=(m // tm,),
        in_specs=[row(3 * w),
                  pl.BlockSpec((1, w), lambda i: (0, 0)), pl.BlockSpec((1, w), lambda i: (0, 0)),
                  pl.BlockSpec((LANES, LANES), lambda i: (0, 0))],
        out_specs=[q_spec, row(w), row(w), row(w), v_spec],
        out_shape=[jax.ShapeDtypeStruct(q_shape, BF16), jax.ShapeDtypeStruct((m, w), F32),
                   jax.ShapeDtypeStruct((m, w), BF16), jax.ShapeDtypeStruct((m, w), F32),
                   jax.ShapeDtypeStruct(v_shape, BF16)],
        compiler_params=_cparams("arbitrary"),
        name="qk_norm",
    )(u_qkv, qg, kg, sel)


def _lambda(lq1_ref, lk1_ref, lq2_ref, lk2_ref, lam_init):
    s1 = jnp.sum(lq1_ref[...] * lk1_ref[...], axis=-1, keepdims=True)
    s2 = jnp.sum(lq2_ref[...] * lk2_ref[...], axis=-1, keepdims=True)
    return jnp.exp(s1) - jnp.exp(s2) + lam_init


def _sub_norm(o, subln, lam_init):
    ms = jnp.mean(o * o, axis=-1, keepdims=True)
    return o * lax.rsqrt(ms + NORM_EPS) * subln * (1.0 - lam_init)


def _flash_kernel(rb_ref, q_ref, k_ref, v_ref, lq1_ref, lk1_ref, lq2_ref, lk2_ref, subln_ref, o_ref,
                  m_scr, l_scr, acc_scr, bias_scr, *, t, lam_init):
    h = pl.program_id(0)
    i = pl.program_id(1)

    @pl.when(i == 0)
    def _():
        kk = lax.broadcasted_iota(jnp.int32, (t, t), 0)
        qq = lax.broadcasted_iota(jnp.int32, (t, t), 1)
        table = lambda b: rb_ref[b, h]
        bias_scr[0] = jnp.where(qq >= kk, _rel_bias_minus_far(qq - kk, table), NEG_INF)
        bias_scr[1] = _rel_bias_minus_far(qq - kk + t, table)

    m_scr[...] = jnp.full(m_scr.shape, NEG_INF, F32)
    l_scr[...] = jnp.zeros(l_scr.shape, F32)
    acc_scr[...] = jnp.zeros(acc_scr.shape, F32)

    def step(j, bias):
        start = pl.multiple_of(j * t, t)
        k = k_ref[pl.ds(start, t), :]
        sc = [_dot(k, q_ref[mp]) for mp in range(2)]
        ps = []
        for mp in range(2):
            x = sc[mp] if bias is None else sc[mp] + bias
            m_prev = m_scr[mp]
            m_new = jnp.maximum(m_prev, jnp.max(x, axis=0, keepdims=True))
            alpha = jnp.exp(m_prev - m_new)
            p = jnp.exp(x - m_new)
            l_scr[mp] = alpha * l_scr[mp] + jnp.sum(p, axis=0, keepdims=True)
            acc_scr[mp] = alpha * acc_scr[mp]
            m_scr[mp] = m_new
            ps.append(p.astype(BF16))
        v = v_ref[:, pl.ds(start, t)]
        for mp in range(2):
            acc_scr[mp] += _dot(v, ps[mp])

    def far_block(j, carry):
        step(j, None)
        return carry

    lax.fori_loop(0, i - 1, far_block, 0)

    @pl.when(i > 0)
    def _():
        step(i - 1, bias_scr[1])

    step(i, bias_scr[0])
    lam = _lambda(lq1_ref, lk1_ref, lq2_ref, lk2_ref, lam_init)
    o = acc_scr[0] / l_scr[0] - lam * (acc_scr[1] / l_scr[1])
    ms = jnp.mean(o * o, axis=0, keepdims=True)
    o = o * lax.rsqrt(ms + NORM_EPS) * subln_ref[...] * (1.0 - lam_init)
    o_ref[...] = o.T.astype(o_ref.dtype)


def _flash_prompt(qt2, kb, vt, rel_bias, lam_p, subln, lam_init, t):
    n_tok = kb.shape[0]
    nb = n_tok // t
    hd = 2 * B_QK_DIM
    vec = lambda n: pl.BlockSpec((1, n), lambda h, i: (0, 0))
    kern = functools.partial(_flash_kernel, t=t, lam_init=lam_init)
    return pl.pallas_call(
        kern,
        grid=(B_HEADS, nb),
        in_specs=[pl.BlockSpec(memory_space=pltpu.SMEM),
                  pl.BlockSpec((2, hd, t), lambda h, i: (0, h, i)),
                  pl.BlockSpec((n_tok, hd), lambda h, i: (0, h)),
                  pl.BlockSpec((B_V_DIM, n_tok), lambda h, i: (h, 0)),
                  vec(B_QK_DIM), vec(B_QK_DIM), vec(B_QK_DIM), vec(B_QK_DIM),
                  pl.BlockSpec((B_V_DIM, 1), lambda h, i: (0, 0))],
        out_specs=pl.BlockSpec((t, B_V_DIM), lambda h, i: (i, h)),
        out_shape=jax.ShapeDtypeStruct((n_tok, B_V_WIDTH), BF16),
        scratch_shapes=[pltpu.VMEM((2, 1, t), F32), pltpu.VMEM((2, 1, t), F32),
                        pltpu.VMEM((2, B_V_DIM, t), F32), pltpu.VMEM((2, t, t), F32)],
        compiler_params=_cparams("arbitrary", "arbitrary"),
        name="diff_flash_attention",
    )(rel_bias, qt2, kb, vt, *lam_p, subln.reshape(-1, 1))


def _decode_kernel(pt_ref, q_ref, kn_ref, vn_ref, rbrow_ref, lq1_ref, lk1_ref, lq2_ref, lk2_ref, subln_ref,
                   *rest, pages_per_step, page_size, n_new, lam_init):
    P = pages_per_step
    k_refs = rest[0:P]
    v_refs = rest[P:2 * P]
    o_ref = rest[2 * P]
    m_scr, l_scr, acc_scr = rest[2 * P + 1:]
    j = pl.program_id(1)
    last = pl.num_programs(1) - 1
    R = q_ref.shape[0]
    rows_per_map = B_HEADS * n_new
    L = page_size * B_HEADS

    @pl.when(j == 0)
    def _():
        m_scr[...] = jnp.full(m_scr.shape, NEG_INF, F32)
        l_scr[...] = jnp.zeros(l_scr.shape, F32)
        acc_scr[...] = jnp.zeros(acc_scr.shape, F32)

    q = q_ref[...]

    def row_head_query(width):
        r = lax.broadcasted_iota(jnp.int32, (R, width), 0) % rows_per_map
        return r // n_new, r % n_new

    col = lax.broadcasted_iota(jnp.int32, (R, L), 1)
    head, qi = row_head_query(L)
    own_head = (col % B_HEADS) == head

    def page_scores(pi, with_bias):
        k2 = k_refs[pi][...].reshape(L, 2 * B_QK_DIM).astype(BF16)
        s = _dot_nt(q, k2)
        if with_bias:
            dist = qi + page_size - col // B_HEADS
            s = s + _rel_bias_minus_far(dist, lambda b: rbrow_ref[:, b:b + 1])
        return jnp.where(own_head, s, NEG_INF)

    def attend(scores, values):
        m_prev = m_scr[...]
        m_new = m_prev
        for s in scores:
            m_new = jnp.maximum(m_new, jnp.max(s, axis=-1, keepdims=True))
        alpha = jnp.exp(m_prev - m_new)
        ps = [jnp.exp(s - m_new) for s in scores]
        l_new = alpha * l_scr[...]
        acc = alpha * acc_scr[...]
        for p in ps:
            l_new = l_new + jnp.sum(p, axis=-1, keepdims=True)
        for p, v in zip(ps, values):
            acc = acc + _dot(p.astype(BF16), v)
        l_scr[...] = l_new
        acc_scr[...] = acc
        m_scr[...] = m_new

    def page_values():
        return [v_refs[pi][...].reshape(L, B_V_DIM).astype(BF16) for pi in range(P)]

    @pl.when(j != last)
    def _():
        attend([page_scores(pi, False) for pi in range(P)], page_values())

    @pl.when(j == last)
    def _():
        scores = [page_scores(pi, pi == P - 1) for pi in range(P)]
        kn = kn_ref[...]
        s = _dot_nt(q, kn)
        nl = kn.shape[0]
        ncol = lax.broadcasted_iota(jnp.int32, (R, nl), 1)
        nhead, nqi = row_head_query(nl)
        tok = ncol // B_HEADS
        s = s + _rel_bias_minus_far(nqi - tok, lambda b: rbrow_ref[:, b:b + 1])
        s = jnp.where((ncol % B_HEADS == nhead) & (tok <= nqi), s, NEG_INF)
        attend(scores + [s], page_values() + [vn_ref[...]])
        lam = _lambda(lq1_ref, lk1_ref, lq2_ref, lk2_ref, lam_init)
        o = acc_scr[...] / l_scr[...]
        o = o[0:rows_per_map] - lam * o[rows_per_map:2 * rows_per_map]
        o_ref[...] = _sub_norm(o, subln_ref[...], lam_init).astype(o_ref.dtype)


def _decode_attention(page_table, qd, kn, vn, rbrow, lam_p, subln, cache_k, cache_v, layer, lam_init, n_new,
                      pages_per_step):
    nseq, n_pages = page_table.shape
    page_size = cache_k.shape[2]
    P = pages_per_step
    n_steps = n_pages // P
    R = qd.shape[1]
    hd = 2 * B_QK_DIM
    vec = lambda n: pl.BlockSpec((1, n), lambda b, j, pt: (0, 0))

    def page_spec(pi, width):
        return pl.BlockSpec((None, None, page_size, B_HEADS, width),
                            lambda b, j, pt: (layer, pt[b, j * P + pi], 0, 0, 0))

    kern = functools.partial(_decode_kernel, pages_per_step=P, page_size=page_size, n_new=n_new, lam_init=lam_init)
    grid_spec = pltpu.PrefetchScalarGridSpec(
        num_scalar_prefetch=1,
        grid=(nseq, n_steps),
        in_specs=[pl.BlockSpec((None, R, hd), lambda b, j, pt: (b, 0, 0)),
                  pl.BlockSpec((None, kn.shape[1], hd), lambda b, j, pt: (b, 0, 0)),
                  pl.BlockSpec((None, vn.shape[1], B_V_DIM), lambda b, j, pt: (b, 0, 0)),
                  pl.BlockSpec(rbrow.shape, lambda b, j, pt: (0, 0)),
                  vec(B_QK_DIM), vec(B_QK_DIM), vec(B_QK_DIM), vec(B_QK_DIM), vec(B_V_DIM)]
                 + [page_spec(pi, hd) for pi in range(P)] + [page_spec(pi, B_V_DIM) for pi in range(P)],
        out_specs=pl.BlockSpec((None, R // 2, B_V_DIM), lambda b, j, pt: (b, 0, 0)),
        scratch_shapes=[pltpu.VMEM((R, 1), F32), pltpu.VMEM((R, 1), F32), pltpu.VMEM((R, B_V_DIM), F32)],
    )
    return pl.pallas_call(
        kern,
        grid_spec=grid_spec,
        out_shape=jax.ShapeDtypeStruct((nseq, R // 2, B_V_DIM), BF16),
        compiler_params=_cparams("arbitrary", "arbitrary"),
        name="diff_paged_decode",
    )(page_table, qd, kn, vn, rbrow, *lam_p, subln, *([cache_k] * P), *([cache_v] * P))


def _merge_kernel(ya_ref, yb_ref, ga_ref, gb_ref, wa_ref, wb_ref, o_ref):
    ma = _dot(ya_ref[...], wa_ref[...])
    mb = _dot(yb_ref[...], wb_ref[...])
    o_ref[...] = (_sigmoid(ga_ref[...]) * ma + _sigmoid(gb_ref[...]) * mb).astype(o_ref.dtype)


def _merge(ya, yb, gates, wa, wb, tm):
    m = ya.shape[0]
    d = wa.shape[1]
    return pl.pallas_call(
        _merge_kernel,
        grid=(m // tm,),
        in_specs=[pl.BlockSpec((tm, ya.shape[1]), lambda i: (i, 0)),
                  pl.BlockSpec((tm, yb.shape[1]), lambda i: (i, 0)),
                  pl.BlockSpec((tm, d), lambda i: (i, 0)),
                  pl.BlockSpec((tm, d), lambda i: (i, 1)),
                  pl.BlockSpec(wa.shape, lambda i: (0, 0)),
                  pl.BlockSpec(wb.shape, lambda i: (0, 0))],
        out_specs=pl.BlockSpec((tm, d), lambda i: (i, 0)),
        out_shape=jax.ShapeDtypeStruct((m, d), BF16),
        compiler_params=_cparams("arbitrary"),
        name="branch_merge",
    )(ya, yb, gates, gates, wa, wb)


def _out_router_kernel(mg_ref, x_ref, g1_ref, gain_ref, scale_ref, shift_ref, wo_ref, wrh_ref, wrm_ref, br_ref,
                       cnt0_ref, x1_ref, h2_ref, meta_ref, cnt_ref, carry_scr):
    @pl.when(pl.program_id(0) == 0)
    def _():
        carry_scr[...] = cnt0_ref[...]

    x1 = x_ref[...] + g1_ref[...] * _dot(mg_ref[...], wo_ref[...])
    x1_ref[...] = x1
    h2 = _modnorm(x1, gain_ref[...], scale_ref[...], shift_ref[...])
    h2_ref[...] = h2
    hi, mid, _ = _split3(h2)
    logit = _dot(hi, wrh_ref[...]) + _dot(hi, wrm_ref[...]) + _dot(mid, wrh_ref[...]) + br_ref[...]
    lane = lax.broadcasted_iota(jnp.int32, logit.shape, 1)
    big = jnp.int32(2 ** 30)

    def first_max(mask):
        v = jnp.max(jnp.where(mask, logit, NEG_INF), axis=-1, keepdims=True)
        idx = jnp.min(jnp.where(mask & (logit == v), lane, big), axis=-1, keepdims=True)
        return v, idx

    coarse = lane < N_GROUPS
    vc, g_idx = first_max(coarse)
    p_g = 1.0 / jnp.sum(jnp.where(coarse, jnp.exp(logit - vc), 0.0), axis=-1, keepdims=True)
    lo = N_GROUPS + g_idx * EXPERTS_PER_GROUP
    fine = (lane >= lo) & (lane < lo + EXPERTS_PER_GROUP)
    v1, i1 = first_max(fine)
    v2, i2 = first_max(fine & (lane != i1))
    e2 = jnp.exp(v2 - v1)
    p1 = 1.0 / (1.0 + e2)
    tm = logit.shape[0]
    member = ((lane == i1) | (lane == i2)).astype(BF16)
    rr = lax.broadcasted_iota(jnp.int32, (tm, tm), 0)
    cc = lax.broadcasted_iota(jnp.int32, (tm, tm), 1)
    rank = carry_scr[...] + _dot((cc < rr).astype(BF16), member)
    carry = carry_scr[...] + jnp.sum(member.astype(F32), axis=0, keepdims=True)
    carry_scr[...] = carry
    cnt_ref[...] = carry
    pick = lambda mask, x: jnp.sum(jnp.where(mask, x, 0.0), axis=-1, keepdims=True)
    fields = [(i1 - N_GROUPS).astype(F32), (i2 - N_GROUPS).astype(F32), pick(lane == i1, rank), pick(lane == i2, rank),
              p_g * p1, p_g * (e2 * p1)]
    meta = jnp.zeros(logit.shape, F32)
    for n, f in enumerate(fields):
        meta = jnp.where(lane == n, f, meta)
    meta_ref[...] = meta


META_E1, META_E2, META_R1, META_R2, META_W1, META_W2 = range(6)


def _out_router(mg, x, g1, gain, scale, shift, wo, wrh, wrm, br, cnt0, tm):
    m, d = x.shape
    full = lambda a: pl.BlockSpec(a.shape, lambda i: (0, 0))
    row = lambda n: pl.BlockSpec((tm, n), lambda i: (i, 0))
    return pl.pallas_call(
        _out_router_kernel,
        grid=(m // tm,),
        in_specs=[row(d), row(d), _mod_spec(g1, tm), full(gain), _mod_spec(scale, tm), _mod_spec(shift, tm),
                  full(wo), full(wrh), full(wrm), full(br), full(cnt0)],
        out_specs=[row(d), row(d), row(ROUTER_LANES), pl.BlockSpec((1, ROUTER_LANES), lambda i: (0, 0))],
        out_shape=[jax.ShapeDtypeStruct((m, d), F32), jax.ShapeDtypeStruct((m, d), F32),
                   jax.ShapeDtypeStruct((m, ROUTER_LANES), F32), jax.ShapeDtypeStruct((1, ROUTER_LANES), F32)],
        scratch_shapes=[pltpu.VMEM((1, ROUTER_LANES), F32)],
        compiler_params=_cparams("arbitrary"),
        name="out_proj_router",
    )(mg, x, g1, gain, scale, shift, wo, wrh, wrm, br, cnt0)


EXPERT_TILE = 256


def _row_copy(src_ref, src_row, dst_ref, dst_row, sem):
    return pltpu.make_async_copy(src_ref.at[pl.ds(src_row, 1)], dst_ref.at[pl.ds(dst_row, 1)], sem)


def _dispatch_kernel(slots_ref, pad_ref, h_ref, *rest, first):
    if first:
        xs_ref, sem, zero_buf = rest
    else:
        _, xs_ref, sem = rest
    tb = h_ref.shape[0]
    base = pl.program_id(0) * tb

    if first:
        @pl.when(pl.program_id(0) == 0)
        def _():
            zero_buf[...] = jnp.zeros(zero_buf.shape, F32)

            def zero_copy(e):
                row = pl.multiple_of(jnp.maximum(pad_ref[e], 0), EXPERT_TILE)
                return pltpu.make_async_copy(zero_buf, xs_ref.at[pl.ds(row, EXPERT_TILE)], sem)

            def start(e, c):
                @pl.when(pad_ref[e] >= 0)
                def _():
                    zero_copy(e).start()
                return c

            def wait(e, c):
                @pl.when(pad_ref[e] >= 0)
                def _():
                    zero_copy(e).wait()
                return c

            lax.fori_loop(0, pad_ref.shape[0], start, 0)
            lax.fori_loop(0, pad_ref.shape[0], wait, 0)

    def issue(t, c):
        for k in range(2):
            _row_copy(h_ref, t, xs_ref, slots_ref[k * (slots_ref.shape[0] // 2) + base + t], sem).start()
        return c

    def drain(t, c):
        for k in range(2):
            _row_copy(h_ref, 0, xs_ref, 0, sem).wait()
        return c

    lax.fori_loop(0, tb, issue, 0, unroll=8)
    lax.fori_loop(0, tb, drain, 0, unroll=8)


def _moe_dispatch(h2, slots, pad_rows, xs, n_rows, tb):
    m, d = h2.shape
    first = xs is None
    grid_spec = pltpu.PrefetchScalarGridSpec(
        num_scalar_prefetch=2,
        grid=(m // tb,),
        in_specs=[pl.BlockSpec((tb, d), lambda i, s, z: (i, 0))] + ([] if first else [pl.BlockSpec(memory_space=pl.ANY)]),
        out_specs=pl.BlockSpec(memory_space=pl.ANY),
        scratch_shapes=[pltpu.SemaphoreType.DMA(())] + ([pltpu.VMEM((EXPERT_TILE, d), F32)] if first else []),
    )
    return pl.pallas_call(
        functools.partial(_dispatch_kernel, first=first),
        grid_spec=grid_spec,
        out_shape=jax.ShapeDtypeStruct((n_rows, d), F32),
        input_output_aliases={} if first else {3: 0},
        compiler_params=_cparams("arbitrary"),
        name="moe_dispatch",
    )(slots, pad_rows, h2, *([] if first else [xs]))


def _expert_kernel(te_ref, nu_ref, x_ref, wg_ref, wu_ref, wd_ref, y_ref):
    t = pl.program_id(0)

    @pl.when(t < nu_ref[0])
    def _():
        x = x_ref[...].astype(BF16)
        hg = _dot(x, wg_ref[...].astype(BF16))
        hu = _dot(x, wu_ref[...].astype(BF16))
        act = _silu(hg) * hu
        y_ref[...] = _dot(act.astype(BF16), wd_ref[...].astype(BF16))

    @pl.when(t >= nu_ref[0])
    def _():
        y_ref[...] = jnp.zeros(y_ref.shape, F32)


def _moe_experts(xs, tile_expert, n_used, wg, wu, wd):
    n_rows, d = xs.shape
    f = wg.shape[2]
    n_tiles = n_rows // EXPERT_TILE
    used = lambda t, nu: jnp.minimum(t, nu[0] - 1)
    grid_spec = pltpu.PrefetchScalarGridSpec(
        num_scalar_prefetch=2,
        grid=(n_tiles,),
        in_specs=[pl.BlockSpec((EXPERT_TILE, d), lambda t, te, nu: (used(t, nu), 0)),
                  pl.BlockSpec((None, d, f), lambda t, te, nu: (te[t], 0, 0)),
                  pl.BlockSpec((None, d, f), lambda t, te, nu: (te[t], 0, 0)),
                  pl.BlockSpec((None, f, d), lambda t, te, nu: (te[t], 0, 0))],
        out_specs=pl.BlockSpec((EXPERT_TILE, d), lambda t, te, nu: (t, 0)),
    )
    return pl.pallas_call(
        _expert_kernel,
        grid_spec=grid_spec,
        out_shape=jax.ShapeDtypeStruct((n_rows, d), F32),
        compiler_params=_cparams("arbitrary"),
        name="moe_experts",
    )(tile_expert, n_used, xs, wg, wu, wd)


def _combine_kernel(slots_ref, ys_ref, meta_ref, x1_ref, g2_ref, o_ref, ybuf, sem):
    tb = x1_ref.shape[0]
    i = pl.program_id(0)

    def issue_block(blk, slot):
        def issue(t, c):
            for k in range(2):
                src = slots_ref[k * (slots_ref.shape[0] // 2) + blk * tb + t]
                _row_copy(ys_ref, src, ybuf.at[slot, k], t, sem.at[slot]).start()
            return c

        lax.fori_loop(0, tb, issue, 0, unroll=8)

    @pl.when(i == 0)
    def _():
        issue_block(0, 0)

    @pl.when(i + 1 < pl.num_programs(0))
    def _():
        issue_block(i + 1, lax.rem(i + 1, 2))

    slot = lax.rem(i, 2)

    def drain(t, c):
        for k in range(2):
            _row_copy(ys_ref, 0, ybuf.at[slot, k], 0, sem.at[slot]).wait()
        return c

    lax.fori_loop(0, tb, drain, 0, unroll=8)
    meta = meta_ref[...]
    lane = lax.broadcasted_iota(jnp.int32, meta.shape, 1)
    w1 = jnp.sum(jnp.where(lane == META_W1, meta, 0.0), axis=-1, keepdims=True)
    w2 = jnp.sum(jnp.where(lane == META_W2, meta, 0.0), axis=-1, keepdims=True)
    y = ybuf[slot]
    o_ref[...] = x1_ref[...] + g2_ref[...] * (w1 * y[0] + w2 * y[1])


def _moe_combine(ys, slots, meta, x1, g2, tb):
    m, d = x1.shape
    grid_spec = pltpu.PrefetchScalarGridSpec(
        num_scalar_prefetch=1,
        grid=(m // tb,),
        in_specs=[pl.BlockSpec(memory_space=pl.ANY),
                  pl.BlockSpec((tb, ROUTER_LANES), lambda i, s: (i, 0)),
                  pl.BlockSpec((tb, d), lambda i, s: (i, 0)),
                  _mod_spec(g2, tb)],
        out_specs=pl.BlockSpec((tb, d), lambda i, s: (i, 0)),
        scratch_shapes=[pltpu.VMEM((2, 2, tb, d), F32), pltpu.SemaphoreType.DMA((2,))],
    )
    return pl.pallas_call(
        _combine_kernel,
        grid_spec=grid_spec,
        out_shape=jax.ShapeDtypeStruct((m, d), F32),
        compiler_params=_cparams("arbitrary"),
        name="moe_combine",
    )(slots, ys, meta, x1, g2)


def _moe_sorted(groups, cnt, wg, wu, wd):
    m = sum(g["h2"].shape[0] for g in groups)
    n_tiles = -(-(2 * m) // EXPERT_TILE) + N_EXPERTS
    counts = cnt[0, N_GROUPS:N_GROUPS + N_EXPERTS].astype(jnp.int32)
    padded = (counts + EXPERT_TILE - 1) // EXPERT_TILE * EXPERT_TILE
    seg_end = jnp.cumsum(padded)
    seg_start = seg_end - padded
    n_used = (seg_end[-1] // EXPERT_TILE).astype(jnp.int32).reshape(1)
    tile = jnp.minimum(jnp.arange(n_tiles, dtype=jnp.int32), n_used[0] - 1)
    tile_expert = jnp.sum(seg_end[None, :] // EXPERT_TILE <= tile[:, None], axis=1).astype(jnp.int32)
    all_tiles = jnp.arange(n_tiles, dtype=jnp.int32)
    pad_rows = jnp.concatenate([jnp.where(padded > 0, seg_end - EXPERT_TILE, -1),
                                jnp.where(all_tiles >= n_used[0], all_tiles * EXPERT_TILE, -1)]).astype(jnp.int32)
    xs = None
    for g in groups:
        col = lambda c: g["meta"][:, c].astype(jnp.int32)
        g["slots"] = jnp.concatenate([seg_start[col(META_E1)] + col(META_R1), seg_start[col(META_E2)] + col(META_R2)])
        xs = _moe_dispatch(g["h2"], g["slots"], pad_rows, xs, n_tiles * EXPERT_TILE, g["tb"])
    ys = _moe_experts(xs, tile_expert, n_used, wg, wu, wd)
    return [_moe_combine(ys, g["slots"], g["meta"], g["x1"], g["g2"], g["tb"]) for g in groups]


def _pick(m, prefs):
    for t in prefs:
        if m % t == 0:
            return t
    return m


def _prep_layer_params(l, w_in, mu_shift, w0, w_decay_up, a0, w_iclr_up, w_gate_up, k_k, k_a, r_k, ln_x_w, ln_x_b,
                       q_norm, k_norm, lambda_q1, lambda_k1, lambda_q2, lambda_k2, subln, w_branch_a, w_branch_b,
                       w_out, w_router_coarse, b_router_coarse, w_router_fine, b_router_fine,
                       w_exp_gate, w_exp_up, w_exp_down, norm1, norm2):
    d = w_in.shape[1]
    wi = w_in[l]
    c_qkv = A_COLS + 2 * B_QK_WIDTH + B_V_WIDTH
    w_lora = jnp.zeros((LORA_IN, 3 * A_WIDTH), F32)
    w_lora = w_lora.at[0:DECAY_LORA, 0:A_WIDTH].set(w_decay_up[l])
    w_lora = w_lora.at[DECAY_LORA:DECAY_LORA + ICLR_LORA, A_WIDTH:2 * A_WIDTH].set(w_iclr_up[l])
    w_lora = w_lora.at[DECAY_LORA + ICLR_LORA:, 2 * A_WIDTH:].set(w_gate_up[l])
    lane = jnp.arange(LANES)
    sel64 = (lane[:, None] // A_HEAD_DIM == lane[None, :] // A_HEAD_DIM).astype(BF16)
    w_r = jnp.zeros((d, ROUTER_LANES), F32)
    w_r = w_r.at[:, 0:N_GROUPS].set(w_router_coarse[l])
    w_r = w_r.at[:, N_GROUPS:N_GROUPS + N_EXPERTS].set(
        jnp.transpose(w_router_fine[l], (1, 0, 2)).reshape(d, N_EXPERTS))
    w_rh = w_r.astype(BF16)
    w_rm = (w_r - w_rh.astype(F32)).astype(BF16)
    b_r = jnp.zeros((1, ROUTER_LANES), F32)
    b_r = b_r.at[0, 0:N_GROUPS].set(b_router_coarse[l])
    b_r = b_r.at[0, N_GROUPS:N_GROUPS + N_EXPERTS].set(b_router_fine[l].reshape(-1))
    row = lambda a: a.reshape(1, -1)
    return dict(
        w_in=wi.astype(BF16),
        mu_shift=row(mu_shift[l]), w_lora=w_lora.astype(BF16), w0=row(w0[l]), a0=row(a0[l]), k_k=row(k_k[l]),
        k_a=row(k_a[l]), r_k=row(r_k[l]), ln_x_w=row(ln_x_w[l]), ln_x_b=row(ln_x_b[l]), sel64=sel64,
        q_gain=row(jnp.tile(q_norm[l].reshape(-1), B_HEADS)), k_gain=row(jnp.tile(k_norm[l].reshape(-1), B_HEADS)),
        lam=(row(lambda_q1[l]), row(lambda_k1[l]), row(lambda_q2[l]), row(lambda_k2[l])), subln=row(subln[l]),
        w_a=w_branch_a[l].astype(BF16), w_b=w_branch_b[l].astype(BF16), w_out=w_out[l].astype(BF16),
        w_rh=w_rh, w_rm=w_rm, b_r=b_r,
        w_eg=w_exp_gate[l], w_eu=w_exp_up[l], w_ed=w_exp_down[l],
        norm1=row(norm1[l]), norm2=row(norm2[l]))


def _trunk_rows(x2, mods, p, tm):
    sh1, sc1 = mods[0], mods[1]
    tm_a = _pick(x2.shape[0], (tm,))
    c_qkv = 2 * B_QK_WIDTH + B_V_WIDTH
    h = _modnorm_rows(x2, p["norm1"], sc1, sh1, min(tm_a, 512))
    ua = _proj(h, p["w_in"], 0, A_COLS, tm_a, A_COLS // 2)
    uqkv = _proj(h, p["w_in"], A_COLS, c_qkv, tm_a, 1024)
    ug = _proj(h, p["w_in"], A_COLS + c_qkv, p["w_in"].shape[1] - A_COLS - c_qkv, tm_a, 1024)
    return ua, uqkv, ug


def _tail_rows(x2, ya, yb, ug, mods, p, tm, cnt0):
    _, _, g1, sh2, sc2, g2 = mods
    m = x2.shape[0]
    mg = _merge(ya, yb, ug, p["w_a"], p["w_b"], _pick(m, (tm,)))
    tb = _pick(m, (256, 128))
    x1, h2, meta, cnt = _out_router(mg, x2, g1, p["norm2"], sc2, sh2, p["w_out"], p["w_rh"], p["w_rm"], p["b_r"],
                                    cnt0, tb)
    return dict(h2=h2, meta=meta, x1=x1, g2=g2, tb=tb), cnt


def kernel(x_prompt, x_sample, c_prompt, c_sample, cache_k, cache_v, state_wkv, state_shift, page_table, rel_bias, w_ada, b_ada, norm1, norm2, w_in, mu_shift, w0, w_decay_up, a0, w_iclr_up, w_gate_up, k_k, k_a, r_k, ln_x_w, ln_x_b, q_norm, k_norm, lambda_q1, lambda_k1, lambda_q2, lambda_k2, subln, w_branch_a, w_branch_b, w_out, w_router_coarse, b_router_coarse, w_router_fine, b_router_fine, w_exp_gate, w_exp_up, w_exp_down):
    depth = w_in.shape[0]
    bp, tp, d = x_prompt.shape
    bs, ts, _ = x_sample.shape
    n_new_pad = 8
    kv_new_rows = LANES // B_HEADS
    yp = x_prompt.reshape(bp * tp, d)
    ys = x_sample.reshape(bs * ts, d)
    c_all = jnp.concatenate([c_prompt, c_sample], axis=0)
    c_rows = -(-c_all.shape[0] // 8) * 8
    c_all = jnp.pad(c_all, ((0, c_rows - c_all.shape[0]), (0, 0)))
    outs = [[] for _ in range(8)]
    for l in range(depth):
        lam_init = 0.8 - 0.6 * math.exp(-0.3 * l)
        p = _prep_layer_params(l, w_in, mu_shift, w0, w_decay_up, a0, w_iclr_up, w_gate_up, k_k, k_a, r_k, ln_x_w,
                               ln_x_b, q_norm, k_norm, lambda_q1, lambda_k1, lambda_q2, lambda_k2, subln,
                               w_branch_a, w_branch_b, w_out, w_router_coarse, b_router_coarse, w_router_fine,
                               b_router_fine, w_exp_gate, w_exp_up, w_exp_down, norm1, norm2)
        ada = _ada(c_all, w_ada[l], b_ada[l])
        mods_p = [jnp.repeat(a, tp, axis=0) if bp > 1 else a for a in jnp.split(ada[0:bp], 6, axis=-1)]
        ua, uqkv, ug = _trunk_rows(yp, mods_p, p, _pick(bp * tp, (1024, 512, 256, 128)))
        ua3 = ua.reshape(bp, tp, A_COLS)
        ya, wkv_p = _rwkv(ua3, jnp.zeros((bp, 1, A_COLS), F32),
                          jnp.zeros((bp, A_HEADS, A_HEAD_DIM, A_HEAD_DIM), F32), p, 64,
                          2 if tp % 128 == 0 else 1, None)
        qt2, kf, kb, vf, vt = _qk_prep(uqkv, p["q_gain"], p["k_gain"], p["sel64"], 512, True)
        t_blk = _pick(tp, (512, 256, 128))
        yb = jnp.concatenate(
            [_flash_prompt(qt2[:, :, b * tp:(b + 1) * tp], kb[b * tp:(b + 1) * tp], vt[:, b * tp:(b + 1) * tp],
                           rel_bias, p["lam"], p["subln"], lam_init, t_blk) for b in range(bp)], axis=0)
        grp_p, cnt_p = _tail_rows(yp, ya.reshape(bp * tp, A_WIDTH), yb, ug, mods_p, p, 512,
                                  jnp.zeros((1, ROUTER_LANES), F32))
        outs[0].append(kf.reshape(bp, tp, B_HEADS, 2 * B_QK_DIM))
        outs[1].append(vf.reshape(bp, tp, B_HEADS, B_V_DIM))
        outs[2].append(wkv_p)
        outs[3].append(ua3[:, -1])
        mods_s = [jnp.repeat(a, ts, axis=0) for a in jnp.split(ada[bp:bp + bs], 6, axis=-1)]
        ua, uqkv, ug = _trunk_rows(ys, mods_s, p, 128)
        ua3 = ua.reshape(bs, ts, A_COLS)
        ua_pad = jnp.pad(ua3, ((0, 0), (0, n_new_pad - ts), (0, 0)))
        ya, wkv_s = _rwkv(ua_pad, state_shift[l][:, None, :], state_wkv[l], p, n_new_pad, 1, ts)
        ya = ya[:, 0:ts].reshape(bs * ts, A_WIDTH)
        q2, kf, kb, vf, vb = _qk_prep(uqkv, p["q_gain"], p["k_gain"], p["sel64"], _pick(bs * ts, (128,)), False)
        qd = q2.reshape(2, bs, ts, B_HEADS, 2 * B_QK_DIM).transpose(1, 0, 3, 2, 4).reshape(
            bs, 2 * B_HEADS * ts, 2 * B_QK_DIM)
        kn = jnp.pad(kb.reshape(bs, ts * B_HEADS, 2 * B_QK_DIM), ((0, 0), (0, (kv_new_rows - ts) * B_HEADS), (0, 0)))
        vn = jnp.pad(vb.reshape(bs, ts * B_HEADS, B_V_DIM), ((0, 0), (0, (kv_new_rows - ts) * B_HEADS), (0, 0)))
        rbrow = jnp.tile(jnp.repeat(rel_bias.T, ts, axis=0), (2, 1))
        od = _decode_attention(page_table, qd, kn, vn, rbrow, p["lam"], p["subln"], cache_k, cache_v, l, lam_init,
                               ts, _pick(page_table.shape[1], (16, 8, 4, 2, 1)))
        yb = od.reshape(bs, B_HEADS, ts, B_V_DIM).transpose(0, 2, 1, 3).reshape(bs * ts, B_V_WIDTH)
        grp_s, cnt_all = _tail_rows(ys, ya, yb, ug, mods_s, p, 128, cnt_p)
        yp, ys = _moe_sorted([grp_p, grp_s], cnt_all, p["w_eg"], p["w_eu"], p["w_ed"])
        outs[4].append(kf.reshape(bs, ts, B_HEADS, 2 * B_QK_DIM))
        outs[5].append(vf.reshape(bs, ts, B_HEADS, B_V_DIM))
        outs[6].append(wkv_s)
        outs[7].append(ua3[:, -1])
    st = [jnp.stack(o) for o in outs]
    return (yp.reshape(bp, tp, d), ys.reshape(bs, ts, d), st[0], st[1], st[2], st[3], st[4], st[5], st[6], st[7])
```

```python
import functools
import math

import jax
import jax.numpy as jnp
from jax import lax
from jax.experimental import pallas as pl
from jax.experimental.pallas import tpu as pltpu

F32 = jnp.float32
BF16 = jnp.bfloat16

A_HEADS = 16
A_HEAD_DIM = 64
A_WIDTH = A_HEADS * A_HEAD_DIM
DECAY_LORA = 64
ICLR_LORA = 64
GATE_LORA = 128
LORA_IN = DECAY_LORA + ICLR_LORA + GATE_LORA
A_COLS = 3 * A_WIDTH + LORA_IN
GN_EPS = 64e-5
B_HEADS = 8
B_QK_DIM = 64
B_V_DIM = 2 * B_QK_DIM
B_QK_WIDTH = B_HEADS * 2 * B_QK_DIM
B_V_WIDTH = B_HEADS * B_V_DIM
NUM_BUCKETS = 32
MAX_DISTANCE = 128
N_GROUPS = 4
EXPERTS_PER_GROUP = 8
N_EXPERTS = N_GROUPS * EXPERTS_PER_GROUP
NORM_EPS = 1e-6
NEG_INF = -1e30

LANES = 128
ROUTER_LANES = LANES
VMEM_LIMIT = 56 * 1024 * 1024


def _cparams(*sem):
    return pltpu.CompilerParams(dimension_semantics=sem, vmem_limit_bytes=VMEM_LIMIT)


def _dot(a, b):
    return jnp.dot(a, b, preferred_element_type=F32)


def _dot_nt(a, b):
    return lax.dot_general(a, b, (((1,), (1,)), ((), ())), preferred_element_type=F32)


def _dot_tn(a, b):
    return lax.dot_general(a, b, (((0,), (0,)), ((), ())), preferred_element_type=F32)


def _split3(x):
    hi = x.astype(BF16)
    r1 = x - hi.astype(F32)
    mid = r1.astype(BF16)
    lo = (r1 - mid.astype(F32)).astype(BF16)
    return hi, mid, lo


def _dot_sel_rhs(x, sel):
    hi, mid, lo = _split3(x)
    return _dot(hi, sel) + _dot(mid, sel) + _dot(lo, sel)


def _dot_sel_lhs(sel, x):
    hi, mid, lo = _split3(x)
    return _dot(sel, hi) + _dot(sel, mid) + _dot(sel, lo)


def _group_sum(x, sel):
    hi = x.astype(BF16)
    mid = (x - hi.astype(F32)).astype(BF16)
    parts = [_dot(hi[:, t * LANES:(t + 1) * LANES], sel) + _dot(mid[:, t * LANES:(t + 1) * LANES], sel)
             for t in range(x.shape[1] // LANES)]
    return parts[0] if len(parts) == 1 else jnp.concatenate(parts, axis=1)


def _sigmoid(x):
    return 1.0 / (1.0 + jnp.exp(-x))


def _silu(x):
    return x * _sigmoid(x)


def _rel_bias_minus_far(dist, table_fn):
    n = jnp.maximum(dist, 0)
    max_exact = NUM_BUCKETS // 2
    nf = jnp.maximum(n, 1).astype(F32)
    large = max_exact + (jnp.log(nf / max_exact) / math.log(MAX_DISTANCE / max_exact)
                         * (NUM_BUCKETS - max_exact)).astype(jnp.int32)
    large = jnp.minimum(large, NUM_BUCKETS - 1)
    bucket = jnp.where(n < max_exact, n, large)
    far = table_fn(NUM_BUCKETS - 1)
    out = jnp.zeros(dist.shape, F32)
    for b in range(NUM_BUCKETS - 1):
        out = jnp.where(bucket == b, table_fn(b) - far, out)
    return out


def _ada_kernel(c_ref, w_ref, b_ref, o_ref):
    s = _silu(c_ref[...])
    o_ref[...] = _dot(s.astype(BF16), w_ref[...].astype(BF16)) + b_ref[...]


def _ada(c, w, b):
    m, d = c.shape
    n = w.shape[1]
    tn = 1024
    return pl.pallas_call(
        _ada_kernel,
        grid=(n // tn,),
        in_specs=[pl.BlockSpec((m, d), lambda j: (0, 0)),
                  pl.BlockSpec((d, tn), lambda j: (0, j)),
                  pl.BlockSpec((1, tn), lambda j: (0, j))],
        out_specs=pl.BlockSpec((m, tn), lambda j: (0, j)),
        out_shape=jax.ShapeDtypeStruct((m, n), F32),
        compiler_params=_cparams("arbitrary"),
        name="ada_modulation",
    )(c, w, b.reshape(1, n))


def _modnorm(x, gain, scale, shift):
    ms = jnp.mean(x * x, axis=-1, keepdims=True)
    h = x * lax.rsqrt(ms + NORM_EPS) * gain
    return h * (1.0 + scale) + shift


def _modnorm_kernel(x_ref, gain_ref, scale_ref, shift_ref, h_ref):
    h_ref[...] = _modnorm(x_ref[...], gain_ref[...], scale_ref[...], shift_ref[...]).astype(BF16)


def _proj_kernel(h_ref, w_ref, o_ref):
    o_ref[...] = _dot(h_ref[...], w_ref[...])


def _proj(h, w, col0, n, tm, tn):
    m, d = h.shape
    return pl.pallas_call(
        _proj_kernel,
        grid=(m // tm, n // tn),
        in_specs=[pl.BlockSpec((tm, d), lambda i, j: (i, 0)),
                  pl.BlockSpec((pl.Element(d), pl.Element(tn)),
                               lambda i, j: (0, pl.multiple_of(col0 + j * tn, LANES)))],
        out_specs=pl.BlockSpec((tm, tn), lambda i, j: (i, j)),
        out_shape=jax.ShapeDtypeStruct((m, n), F32),
        compiler_params=_cparams("arbitrary", "arbitrary"),
        name="in_proj",
    )(h, w)


def _mod_spec(mod, tm):
    if mod.shape[0] == 1:
        return pl.BlockSpec((1, mod.shape[1]), lambda i, *_: (0, 0))
    return pl.BlockSpec((tm, mod.shape[1]), lambda i, *_: (i, 0))


def _modnorm_rows(x, gain, scale, shift, tm):
    m, d = x.shape
    return pl.pallas_call(
        _modnorm_kernel,
        grid=(m // tm,),
        in_specs=[pl.BlockSpec((tm, d), lambda i: (i, 0)),
                  pl.BlockSpec((1, d), lambda i: (0, 0)),
                  _mod_spec(scale, tm), _mod_spec(shift, tm)],
        out_specs=pl.BlockSpec((tm, d), lambda i: (i, 0)),
        out_shape=jax.ShapeDtypeStruct((m, d), BF16),
        compiler_params=_cparams("arbitrary"),
        name="modulated_norm",
    )(x, gain, scale, shift)


def _rwkv_kernel(ua_ref, shift0_ref, s0_ref, mu_ref, wl_ref, w0_ref, a0_ref, kk_ref, ka_ref, rk_ref,
                 lnw_ref, lnb_ref, sel_ref, y_ref, sout_ref, state_scr, prev_scr, yh_scr, *, chunk, n_sub, n_valid):
    c = pl.program_id(1)
    n_chunks = pl.num_programs(1)
    C = chunk
    TB = chunk * n_sub
    W = A_WIDTH

    @pl.when(c == 0)
    def _():
        state_scr[...] = s0_ref[...]
        prev_scr[...] = shift0_ref[...]

    ua = ua_ref[...]
    row = lax.broadcasted_iota(jnp.int32, (TB, 1), 0)
    prev = jnp.where(row == 0, prev_scr[...], pltpu.roll(ua, 1, axis=0))
    prev_scr[...] = ua[TB - 1:TB, :]
    us = ua + (prev - ua) * mu_ref[...]
    r = us[:, 0:W]
    kr = us[:, W:2 * W]
    vr = us[:, 2 * W:3 * W]
    lin = us[:, 3 * W:3 * W + LORA_IN]
    lane = lax.broadcasted_iota(jnp.int32, (1, LORA_IN), 1)
    act = jnp.where(lane < DECAY_LORA, jnp.tanh(lin),
                    jnp.where(lane < DECAY_LORA + ICLR_LORA, lin, _sigmoid(lin)))
    lora = _dot(act.astype(BF16), wl_ref[...])
    z = -(w0_ref[...] + lora[:, 0:W])
    softplus = jnp.maximum(z, 0.0) + jnp.log(1.0 + jnp.exp(-jnp.abs(z)))
    ld = -jnp.exp(-softplus - 0.5)
    iclr = _sigmoid(a0_ref[...] + lora[:, W:2 * W])
    gate_o = lora[:, 2 * W:3 * W]

    sel = sel_ref[...]
    kk = kr * kk_ref[...]
    kk = kk / jnp.maximum(jnp.sqrt(_group_sum(kk * kk, sel)), 1e-12)
    k_mod = kr * (1.0 + (iclr - 1.0) * ka_ref[...])
    if n_valid is not None:
        valid = (c * TB + row) < n_valid
        ld = jnp.where(valid, ld, 0.0)
        kk = jnp.where(valid, kk, 0.0)
        k_mod = jnp.where(valid, k_mod, 0.0)
        vr = jnp.where(valid, vr, 0.0)

    ri = lax.broadcasted_iota(jnp.int32, (TB, TB), 0)
    ci = lax.broadcasted_iota(jnp.int32, (TB, TB), 1)
    same_chunk_prefix = (ci <= ri) & ((ci // C) == (ri // C))
    cum = _dot_sel_lhs(same_chunk_prefix.astype(BF16), ld)
    g_inv = jnp.exp(-cum)
    g_last = [jnp.exp(cum[(s + 1) * C - 1:(s + 1) * C, :]) for s in range(n_sub)]
    g_last_rows = jnp.concatenate([jnp.broadcast_to(g, (C, W)) for g in g_last], axis=0)
    at = -kk * jnp.exp(cum - ld)
    bt = kk * iclr * g_inv
    kt = k_mod * g_inv
    bl = bt * g_last_rows
    kl = kt * g_last_rows
    rt = r * jnp.exp(cum)
    rc = lax.broadcasted_iota(jnp.int32, (C, C), 0)
    cc = lax.broadcasted_iota(jnp.int32, (C, C), 1)
    eye = (cc == rc).astype(F32)
    n_double = max(int(math.ceil(math.log2(C))) - 1, 0)
    r2 = lax.broadcasted_iota(jnp.int32, (2 * C, 2 * C), 0)
    c2 = lax.broadcasted_iota(jnp.int32, (2 * C, 2 * C), 1)
    mask2 = ((r2 < C) & ((c2 % C) < r2)) | ((r2 >= C) & ((c2 % C) <= (r2 - C)))
    zeros_c = jnp.zeros((C, A_HEAD_DIM), F32)
    heads = range(A_HEADS)
    items = [(s, h) for s in range(n_sub) for h in heads]

    def hsl(x, it):
        s, h = it
        return x[s * C:(s + 1) * C, h * A_HEAD_DIM:(h + 1) * A_HEAD_DIM]

    def stack(top, bottom):
        return jnp.concatenate([top, bottom], axis=0).astype(BF16)

    atb = {it: hsl(at, it).astype(BF16) for it in items}
    ar = {it: stack(hsl(at, it), hsl(rt, it)) for it in items}
    bk = {it: stack(hsl(bt, it), hsl(kt, it)) for it in items}
    zv = {it: stack(zeros_c, hsl(vr, it)) for it in items}
    bkl = {it: stack(hsl(bl, it), hsl(kl, it)) for it in items}
    m2 = {it: jnp.where(mask2, _dot_nt(ar[it], bk[it]), 0.0) for it in items}
    bvk = {it: _dot(m2[it].astype(BF16), zv[it]) for it in items}
    p = {it: m2[it][0:C, 0:C] for it in items}
    t_inv = {it: eye + p[it] for it in items}
    for _ in range(n_double):
        pb = {it: p[it].astype(BF16) for it in items}
        p = {it: _dot(pb[it], pb[it]) for it in items}
        t_inv = {it: t_inv[it] + _dot(t_inv[it].astype(BF16), p[it].astype(BF16)) for it in items}
    tb = {it: t_inv[it].astype(BF16) for it in items}
    w = {it: _dot(tb[it], atb[it]) for it in items}
    z = {it: _dot(tb[it], bvk[it][0:C].astype(BF16)) for it in items}
    state = [state_scr[h] for h in heads]
    for s in range(n_sub):
        its = [(s, h) for h in heads]
        ws = [_dot_nt(stack(w[it], hsl(rt, it)), state[it[1]].astype(BF16)) for it in its]
        u = [ws[h][0:C] + z[(s, h)] for h in heads]
        y = [ws[h][C:2 * C] + bvk[(s, h)][C:2 * C]
             + _dot(m2[(s, h)][C:2 * C, 0:C].astype(BF16), u[h].astype(BF16)) for h in heads]
        state = [state[h] * g_last[s][:, h * A_HEAD_DIM:(h + 1) * A_HEAD_DIM]
                 + _dot_tn(stack(u[h], hsl(vr, (s, h))), bkl[(s, h)]) for h in heads]
        for h in heads:
            yh_scr[s * C:(s + 1) * C, h * A_HEAD_DIM:(h + 1) * A_HEAD_DIM] = y[h]
    for h in heads:
        state_scr[h] = state[h]

    y = yh_scr[...]
    inv_n = 1.0 / A_HEAD_DIM
    mu_y = _group_sum(y, sel) * inv_n
    dy = y - mu_y
    var = _group_sum(dy * dy, sel) * inv_n
    yn = dy * lax.rsqrt(var + GN_EPS) * lnw_ref[...] + lnb_ref[...]
    bonus = _group_sum(r * k_mod * rk_ref[...], sel) * vr
    y_ref[...] = ((yn + bonus) * gate_o).astype(y_ref.dtype)

    @pl.when(c == n_chunks - 1)
    def _():
        sout_ref[...] = state_scr[...]


def _rwkv(ua, shift0, s0, p, chunk, n_sub, n_valid):
    b, t, _ = ua.shape
    rows = chunk * n_sub
    n_chunks = t // rows
    vec = lambda n: pl.BlockSpec((1, n), lambda i, j: (0, 0))
    kern = functools.partial(_rwkv_kernel, chunk=chunk, n_sub=n_sub, n_valid=n_valid)
    return pl.pallas_call(
        kern,
        grid=(b, n_chunks),
        in_specs=[pl.BlockSpec((None, rows, A_COLS), lambda i, j: (i, j, 0)),
                  pl.BlockSpec((None, 1, A_COLS), lambda i, j: (i, 0, 0)),
                  pl.BlockSpec((None, A_HEADS, A_HEAD_DIM, A_HEAD_DIM), lambda i, j: (i, 0, 0, 0)),
                  vec(A_COLS),
                  pl.BlockSpec((LORA_IN, 3 * A_WIDTH), lambda i, j: (0, 0)),
                  vec(A_WIDTH), vec(A_WIDTH), vec(A_WIDTH), vec(A_WIDTH), vec(A_WIDTH), vec(A_WIDTH), vec(A_WIDTH),
                  pl.BlockSpec((LANES, LANES), lambda i, j: (0, 0))],
        out_specs=[pl.BlockSpec((None, rows, A_WIDTH), lambda i, j: (i, j, 0)),
                   pl.BlockSpec((None, A_HEADS, A_HEAD_DIM, A_HEAD_DIM), lambda i, j: (i, 0, 0, 0))],
        out_shape=[jax.ShapeDtypeStruct((b, t, A_WIDTH), BF16),
                   jax.ShapeDtypeStruct((b, A_HEADS, A_HEAD_DIM, A_HEAD_DIM), F32)],
        scratch_shapes=[pltpu.VMEM((A_HEADS, A_HEAD_DIM, A_HEAD_DIM), F32),
                        pltpu.VMEM((1, A_COLS), F32),
                        pltpu.VMEM((rows, A_WIDTH), F32)],
        compiler_params=_cparams("arbitrary", "arbitrary"),
        name="rwkv7_chunked",
    )(ua, shift0, s0, p["mu_shift"], p["w_lora"], p["w0"], p["a0"], p["k_k"], p["k_a"], p["r_k"],
      p["ln_x_w"], p["ln_x_b"], p["sel64"])


def _qk_prep_kernel(u_ref, qg_ref, kg_ref, sel_ref, q_ref, kf_ref, kb_ref, vf_ref, vb_ref, *, transposed):
    sel = sel_ref[...]
    wq = B_QK_WIDTH
    uq = u_ref[:, 0:wq]
    uk = u_ref[:, wq:2 * wq]
    uv = u_ref[:, 2 * wq:2 * wq + B_V_WIDTH]
    inv_n = 1.0 / B_QK_DIM
    q = uq * lax.rsqrt(_group_sum(uq * uq, sel) * inv_n + NORM_EPS) * qg_ref[...]
    k = uk * lax.rsqrt(_group_sum(uk * uk, sel) * inv_n + NORM_EPS) * kg_ref[...]
    q = q * (B_QK_DIM ** -0.5)
    kf_ref[...] = k
    kb_ref[...] = k.astype(BF16)
    vf_ref[...] = uv
    if transposed:
        qt = q.T
        ch = lax.broadcasted_iota(jnp.int32, (wq, 1), 0)
        first_map = (ch % (2 * B_QK_DIM)) < B_QK_DIM
        q_ref[0] = jnp.where(first_map, qt, 0.0).astype(BF16)
        q_ref[1] = jnp.where(first_map, 0.0, qt).astype(BF16)
        vb_ref[...] = uv.T.astype(BF16)
    else:
        ch = lax.broadcasted_iota(jnp.int32, (1, wq), 1)
        first_map = (ch % (2 * B_QK_DIM)) < B_QK_DIM
        q_ref[0] = jnp.where(first_map, q, 0.0).astype(BF16)
        q_ref[1] = jnp.where(first_map, 0.0, q).astype(BF16)
        vb_ref[...] = uv.astype(BF16)


def _qk_prep(u_qkv, qg, kg, sel, tm, transposed):
    m = u_qkv.shape[0]
    w = B_QK_WIDTH
    row = lambda n: pl.BlockSpec((tm, n), lambda i: (i, 0))
    if transposed:
        q_spec, q_shape = pl.BlockSpec((2, w, tm), lambda i: (0, 0, i)), (2, w, m)
        v_spec, v_shape = pl.BlockSpec((w, tm), lambda i: (0, i)), (w, m)
    else:
        q_spec, q_shape = pl.BlockSpec((2, tm, w), lambda i: (0, i, 0)), (2, m, w)
        v_spec, v_shape = row(w), (m, w)
    return pl.pallas_call(
        functools.partial(_qk_prep_kernel, transposed=transposed),
        grid=(m // tm,),
        in_specs=[row(3 * w),
                  pl.BlockSpec((1, w), lambda i: (0, 0)), pl.BlockSpec((1, w), lambda i: (0, 0)),
                  pl.BlockSpec((LANES, LANES), lambda i: (0, 0))],
        out_specs=[q_spec, row(w), row(w), row(w), v_spec],
        out_shape=[jax.ShapeDtypeStruct(q_shape, BF16), jax.ShapeDtypeStruct((m, w), F32),
                   jax.ShapeDtypeStruct((m, w), BF16), jax.ShapeDtypeStruct((m, w), F32),
                   jax.ShapeDtypeStruct(v_shape, BF16)],
        compiler_params=_cparams("arbitrary"),
        name="qk_norm",
    )(u_qkv, qg, kg, sel)


def _lambda(lq1_ref, lk1_ref, lq2_ref, lk2_ref, lam_init):
    s1 = jnp.sum(lq1_ref[...] * lk1_ref[...], axis=-1, keepdims=True)
    s2 = jnp.sum(lq2_ref[...] * lk2_ref[...], axis=-1, keepdims=True)
    return jnp.exp(s1) - jnp.exp(s2) + lam_init


def _sub_norm(o, subln, lam_init):
    ms = jnp.mean(o * o, axis=-1, keepdims=True)
    return o * lax.rsqrt(ms + NORM_EPS) * subln * (1.0 - lam_init)


def _flash_kernel(rb_ref, q_ref, k_ref, v_ref, lq1_ref, lk1_ref, lq2_ref, lk2_ref, subln_ref, o_ref,
                  m_scr, l_scr, acc_scr, bias_scr, *, t, lam_init):
    h = pl.program_id(0)
    i = pl.program_id(1)

    @pl.when(i == 0)
    def _():
        kk = lax.broadcasted_iota(jnp.int32, (t, t), 0)
        qq = lax.broadcasted_iota(jnp.int32, (t, t), 1)
        table = lambda b: rb_ref[b, h]
        bias_scr[0] = jnp.where(qq >= kk, _rel_bias_minus_far(qq - kk, table), NEG_INF)
        bias_scr[1] = _rel_bias_minus_far(qq - kk + t, table)

    m_scr[...] = jnp.full(m_scr.shape, NEG_INF, F32)
    l_scr[...] = jnp.zeros(l_scr.shape, F32)
    acc_scr[...] = jnp.zeros(acc_scr.shape, F32)

    def step(j, bias):
        start = pl.multiple_of(j * t, t)
        k = k_ref[pl.ds(start, t), :]
        sc = [_dot(k, q_ref[mp]) for mp in range(2)]
        ps = []
        for mp in range(2):
            x = sc[mp] if bias is None else sc[mp] + bias
            m_prev = m_scr[mp]
            m_new = jnp.maximum(m_prev, jnp.max(x, axis=0, keepdims=True))
            alpha = jnp.exp(m_prev - m_new)
            p = jnp.exp(x - m_new)
            l_scr[mp] = alpha * l_scr[mp] + jnp.sum(p, axis=0, keepdims=True)
            acc_scr[mp] = alpha * acc_scr[mp]
            m_scr[mp] = m_new
            ps.append(p.astype(BF16))
        v = v_ref[:, pl.ds(start, t)]
        for mp in range(2):
            acc_scr[mp] += _dot(v, ps[mp])

    def far_block(j, carry):
        step(j, None)
        return carry

    lax.fori_loop(0, i - 1, far_block, 0)

    @pl.when(i > 0)
    def _():
        step(i - 1, bias_scr[1])

    step(i, bias_scr[0])
    lam = _lambda(lq1_ref, lk1_ref, lq2_ref, lk2_ref, lam_init)
    o = acc_scr[0] / l_scr[0] - lam * (acc_scr[1] / l_scr[1])
    ms = jnp.mean(o * o, axis=0, keepdims=True)
    o = o * lax.rsqrt(ms + NORM_EPS) * subln_ref[...] * (1.0 - lam_init)
    o_ref[...] = o.T.astype(o_ref.dtype)


def _flash_prompt(qt2, kb, vt, rel_bias, lam_p, subln, lam_init, t):
    n_tok = kb.shape[0]
    nb = n_tok // t
    hd = 2 * B_QK_DIM
    vec = lambda n: pl.BlockSpec((1, n), lambda h, i: (0, 0))
    kern = functools.partial(_flash_kernel, t=t, lam_init=lam_init)
    return pl.pallas_call(
        kern,
        grid=(B_HEADS, nb),
        in_specs=[pl.BlockSpec(memory_space=pltpu.SMEM),
                  pl.BlockSpec((2, hd, t), lambda h, i: (0, h, i)),
                  pl.BlockSpec((n_tok, hd), lambda h, i: (0, h)),
                  pl.BlockSpec((B_V_DIM, n_tok), lambda h, i: (h, 0)),
                  vec(B_QK_DIM), vec(B_QK_DIM), vec(B_QK_DIM), vec(B_QK_DIM),
                  pl.BlockSpec((B_V_DIM, 1), lambda h, i: (0, 0))],
        out_specs=pl.BlockSpec((t, B_V_DIM), lambda h, i: (i, h)),
        out_shape=jax.ShapeDtypeStruct((n_tok, B_V_WIDTH), BF16),
        scratch_shapes=[pltpu.VMEM((2, 1, t), F32), pltpu.VMEM((2, 1, t), F32),
                        pltpu.VMEM((2, B_V_DIM, t), F32), pltpu.VMEM((2, t, t), F32)],
        compiler_params=_cparams("arbitrary", "arbitrary"),
        name="diff_flash_attention",
    )(rel_bias, qt2, kb, vt, *lam_p, subln.reshape(-1, 1))


def _decode_kernel(pt_ref, q_ref, kn_ref, vn_ref, rbrow_ref, lq1_ref, lk1_ref, lq2_ref, lk2_ref, subln_ref,
                   *rest, pages_per_step, page_size, n_new, lam_init):
    P = pages_per_step
    k_refs = rest[0:P]
    v_refs = rest[P:2 * P]
    o_ref = rest[2 * P]
    m_scr, l_scr, acc_scr = rest[2 * P + 1:]
    j = pl.program_id(1)
    last = pl.num_programs(1) - 1
    R = q_ref.shape[0]
    rows_per_map = B_HEADS * n_new
    L = page_size * B_HEADS

    @pl.when(j == 0)
    def _():
        m_scr[...] = jnp.full(m_scr.shape, NEG_INF, F32)
        l_scr[...] = jnp.zeros(l_scr.shape, F32)
        acc_scr[...] = jnp.zeros(acc_scr.shape, F32)

    q = q_ref[...]

    def row_head_query(width):
        r = lax.broadcasted_iota(jnp.int32, (R, width), 0) % rows_per_map
        return r // n_new, r % n_new

    col = lax.broadcasted_iota(jnp.int32, (R, L), 1)
    head, qi = row_head_query(L)
    own_head = (col % B_HEADS) == head

    def page_scores(pi, with_bias):
        k2 = k_refs[pi][...].reshape(L, 2 * B_QK_DIM).astype(BF16)
        s = _dot_nt(q, k2)
        if with_bias:
            dist = qi + page_size - col // B_HEADS
            s = s + _rel_bias_minus_far(dist, lambda b: rbrow_ref[:, b:b + 1])
        return jnp.where(own_head, s, NEG_INF)

    def attend(scores, values):
        m_prev = m_scr[...]
        m_new = m_prev
        for s in scores:
            m_new = jnp.maximum(m_new, jnp.max(s, axis=-1, keepdims=True))
        alpha = jnp.exp(m_prev - m_new)
        ps = [jnp.exp(s - m_new) for s in scores]
        l_new = alpha * l_scr[...]
        acc = alpha * acc_scr[...]
        for p in ps:
            l_new = l_new + jnp.sum(p, axis=-1, keepdims=True)
        for p, v in zip(ps, values):
            acc = acc + _dot(p.astype(BF16), v)
        l_scr[...] = l_new
        acc_scr[...] = acc
        m_scr[...] = m_new

    def page_values():
        return [v_refs[pi][...].reshape(L, B_V_DIM).astype(BF16) for pi in range(P)]

    @pl.when(j != last)
    def _():
        attend([page_scores(pi, False) for pi in range(P)], page_values())

    @pl.when(j == last)
    def _():
        scores = [page_scores(pi, pi == P - 1) for pi in range(P)]
        kn = kn_ref[...]
        s = _dot_nt(q, kn)
        nl = kn.shape[0]
        ncol = lax.broadcasted_iota(jnp.int32, (R, nl), 1)
        nhead, nqi = row_head_query(nl)
        tok = ncol // B_HEADS
        s = s + _rel_bias_minus_far(nqi - tok, lambda b: rbrow_ref[:, b:b + 1])
        s = jnp.where((ncol % B_HEADS == nhead) & (tok <= nqi), s, NEG_INF)
        attend(scores + [s], page_values() + [vn_ref[...]])
        lam = _lambda(lq1_ref, lk1_ref, lq2_ref, lk2_ref, lam_init)
        o = acc_scr[...] / l_scr[...]
        o = o[0:rows_per_map] - lam * o[rows_per_map:2 * rows_per_map]
        o_ref[...] = _sub_norm(o, subln_ref[...], lam_init).astype(o_ref.dtype)


def _decode_attention(page_table, qd, kn, vn, rbrow, lam_p, subln, cache_k, cache_v, layer, lam_init, n_new,
                      pages_per_step):
    nseq, n_pages = page_table.shape
    page_size = cache_k.shape[2]
    P = pages_per_step
    n_steps = n_pages // P
    R = qd.shape[1]
    hd = 2 * B_QK_DIM
    vec = lambda n: pl.BlockSpec((1, n), lambda b, j, pt: (0, 0))

    def page_spec(pi, width):
        return pl.BlockSpec((None, None, page_size, B_HEADS, width),
                            lambda b, j, pt: (layer, pt[b, j * P + pi], 0, 0, 0))

    kern = functools.partial(_decode_kernel, pages_per_step=P, page_size=page_size, n_new=n_new, lam_init=lam_init)
    grid_spec = pltpu.PrefetchScalarGridSpec(
        num_scalar_prefetch=1,
        grid=(nseq, n_steps),
        in_specs=[pl.BlockSpec((None, R, hd), lambda b, j, pt: (b, 0, 0)),
                  pl.BlockSpec((None, kn.shape[1], hd), lambda b, j, pt: (b, 0, 0)),
                  pl.BlockSpec((None, vn.shape[1], B_V_DIM), lambda b, j, pt: (b, 0, 0)),
                  pl.BlockSpec(rbrow.shape, lambda b, j, pt: (0, 0)),
                  vec(B_QK_DIM), vec(B_QK_DIM), vec(B_QK_DIM), vec(B_QK_DIM), vec(B_V_DIM)]
                 + [page_spec(pi, hd) for pi in range(P)] + [page_spec(pi, B_V_DIM) for pi in range(P)],
        out_specs=pl.BlockSpec((None, R // 2, B_V_DIM), lambda b, j, pt: (b, 0, 0)),
        scratch_shapes=[pltpu.VMEM((R, 1), F32), pltpu.VMEM((R, 1), F32), pltpu.VMEM((R, B_V_DIM), F32)],
    )
    return pl.pallas_call(
        kern,
        grid_spec=grid_spec,
        out_shape=jax.ShapeDtypeStruct((nseq, R // 2, B_V_DIM), BF16),
        compiler_params=_cparams("arbitrary", "arbitrary"),
        name="diff_paged_decode",
    )(page_table, qd, kn, vn, rbrow, *lam_p, subln, *([cache_k] * P), *([cache_v] * P))


def _merge_kernel(ya_ref, yb_ref, ga_ref, gb_ref, wa_ref, wb_ref, o_ref):
    ma = _dot(ya_ref[...], wa_ref[...])
    mb = _dot(yb_ref[...], wb_ref[...])
    o_ref[...] = (_sigmoid(ga_ref[...]) * ma + _sigmoid(gb_ref[...]) * mb).astype(o_ref.dtype)


def _merge(ya, yb, gates, wa, wb, tm):
    m = ya.shape[0]
    d = wa.shape[1]
    return pl.pallas_call(
        _merge_kernel,
        grid=(m // tm,),
        in_specs=[pl.BlockSpec((tm, ya.shape[1]), lambda i: (i, 0)),
                  pl.BlockSpec((tm, yb.shape[1]), lambda i: (i, 0)),
                  pl.BlockSpec((tm, d), lambda i: (i, 0)),
                  pl.BlockSpec((tm, d), lambda i: (i, 1)),
                  pl.BlockSpec(wa.shape, lambda i: (0, 0)),
                  pl.BlockSpec(wb.shape, lambda i: (0, 0))],
        out_specs=pl.BlockSpec((tm, d), lambda i: (i, 0)),
        out_shape=jax.ShapeDtypeStruct((m, d), BF16),
        compiler_params=_cparams("arbitrary"),
        name="branch_merge",
    )(ya, yb, gates, gates, wa, wb)


def _out_router_kernel(mg_ref, x_ref, g1_ref, gain_ref, scale_ref, shift_ref, wo_ref, wrh_ref, wrm_ref, br_ref,
                       cnt0_ref, x1_ref, h2_ref, meta_ref, cnt_ref, metat_ref, carry_scr):
    @pl.when(pl.program_id(0) == 0)
    def _():
        carry_scr[...] = cnt0_ref[...]

    x1 = x_ref[...] + g1_ref[...] * _dot(mg_ref[...], wo_ref[...])
    x1_ref[...] = x1
    h2 = _modnorm(x1, gain_ref[...], scale_ref[...], shift_ref[...])
    h2_ref[...] = h2
    hi, mid, _ = _split3(h2)
    logit = _dot(hi, wrh_ref[...]) + _dot(hi, wrm_ref[...]) + _dot(mid, wrh_ref[...]) + br_ref[...]
    lane = lax.broadcasted_iota(jnp.int32, logit.shape, 1)
    big = jnp.int32(2 ** 30)

    def first_max(mask):
        v = jnp.max(jnp.where(mask, logit, NEG_INF), axis=-1, keepdims=True)
        idx = jnp.min(jnp.where(mask & (logit == v), lane, big), axis=-1, keepdims=True)
        return v, idx

    coarse = lane < N_GROUPS
    vc, g_idx = first_max(coarse)
    p_g = 1.0 / jnp.sum(jnp.where(coarse, jnp.exp(logit - vc), 0.0), axis=-1, keepdims=True)
    lo = N_GROUPS + g_idx * EXPERTS_PER_GROUP
    fine = (lane >= lo) & (lane < lo + EXPERTS_PER_GROUP)
    v1, i1 = first_max(fine)
    v2, i2 = first_max(fine & (lane != i1))
    e2 = jnp.exp(v2 - v1)
    p1 = 1.0 / (1.0 + e2)
    tm = logit.shape[0]
    member = ((lane == i1) | (lane == i2)).astype(BF16)
    rr = lax.broadcasted_iota(jnp.int32, (tm, tm), 0)
    cc = lax.broadcasted_iota(jnp.int32, (tm, tm), 1)
    rank = carry_scr[...] + _dot((cc < rr).astype(BF16), member)
    carry = carry_scr[...] + jnp.sum(member.astype(F32), axis=0, keepdims=True)
    carry_scr[...] = carry
    cnt_ref[...] = carry
    pick = lambda mask, x: jnp.sum(jnp.where(mask, x, 0.0), axis=-1, keepdims=True)
    fields = [(i1 - N_GROUPS).astype(F32), (i2 - N_GROUPS).astype(F32), pick(lane == i1, rank), pick(lane == i2, rank),
              p_g * p1, p_g * (e2 * p1)]
    meta = jnp.zeros(logit.shape, F32)
    for n, f in enumerate(fields):
        meta = jnp.where(lane == n, f, meta)
    meta_ref[...] = meta
    metat_ref[...] = meta.T[0:metat_ref.shape[0], :]


META_E1, META_E2, META_R1, META_R2, META_W1, META_W2 = range(6)


def _out_router(mg, x, g1, gain, scale, shift, wo, wrh, wrm, br, cnt0, tm):
    m, d = x.shape
    full = lambda a: pl.BlockSpec(a.shape, lambda i: (0, 0))
    row = lambda n: pl.BlockSpec((tm, n), lambda i: (i, 0))
    return pl.pallas_call(
        _out_router_kernel,
        grid=(m // tm,),
        in_specs=[row(d), row(d), _mod_spec(g1, tm), full(gain), _mod_spec(scale, tm), _mod_spec(shift, tm),
                  full(wo), full(wrh), full(wrm), full(br), full(cnt0)],
        out_specs=[row(d), row(d), row(ROUTER_LANES), pl.BlockSpec((1, ROUTER_LANES), lambda i: (0, 0)),
                   pl.BlockSpec((8, tm), lambda i: (0, i))],
        out_shape=[jax.ShapeDtypeStruct((m, d), F32), jax.ShapeDtypeStruct((m, d), F32),
                   jax.ShapeDtypeStruct((m, ROUTER_LANES), F32), jax.ShapeDtypeStruct((1, ROUTER_LANES), F32),
                   jax.ShapeDtypeStruct((8, m), F32)],
        scratch_shapes=[pltpu.VMEM((1, ROUTER_LANES), F32)],
        compiler_params=_cparams("arbitrary"),
        name="out_proj_router",
    )(mg, x, g1, gain, scale, shift, wo, wrh, wrm, br, cnt0)


EXPERT_TILE = 256


def _row_copy(src_ref, src_row, dst_ref, dst_row, sem):
    return pltpu.make_async_copy(src_ref.at[pl.ds(src_row, 1)], dst_ref.at[pl.ds(dst_row, 1)], sem)


def _dispatch_kernel(slots_ref, pad_ref, h_ref, *rest, first):
    if first:
        xs_ref, sem, zero_buf = rest
    else:
        _, xs_ref, sem = rest
    tb = h_ref.shape[0]
    base = pl.program_id(0) * tb

    if first:
        @pl.when(pl.program_id(0) == 0)
        def _():
            zero_buf[...] = jnp.zeros(zero_buf.shape, F32)

            def zero_copy(e):
                row = pl.multiple_of(jnp.maximum(pad_ref[e], 0), EXPERT_TILE)
                return pltpu.make_async_copy(zero_buf, xs_ref.at[pl.ds(row, EXPERT_TILE)], sem)

            def start(e, c):
                @pl.when(pad_ref[e] >= 0)
                def _():
                    zero_copy(e).start()
                return c

            def wait(e, c):
                @pl.when(pad_ref[e] >= 0)
                def _():
                    zero_copy(e).wait()
                return c

            lax.fori_loop(0, pad_ref.shape[0], start, 0)
            lax.fori_loop(0, pad_ref.shape[0], wait, 0)

    def issue(t, c):
        for k in range(2):
            _row_copy(h_ref, t, xs_ref, slots_ref[k * (slots_ref.shape[0] // 2) + base + t], sem).start()
        return c

    def drain(t, c):
        for k in range(2):
            _row_copy(h_ref, 0, xs_ref, 0, sem).wait()
        return c

    lax.fori_loop(0, tb, issue, 0, unroll=8)
    lax.fori_loop(0, tb, drain, 0, unroll=8)


def _moe_dispatch(h2, slots, pad_rows, xs, n_rows, tb):
    m, d = h2.shape
    first = xs is None
    grid_spec = pltpu.PrefetchScalarGridSpec(
        num_scalar_prefetch=2,
        grid=(m // tb,),
        in_specs=[pl.BlockSpec((tb, d), lambda i, s, z: (i, 0))] + ([] if first else [pl.BlockSpec(memory_space=pl.ANY)]),
        out_specs=pl.BlockSpec(memory_space=pl.ANY),
        scratch_shapes=[pltpu.SemaphoreType.DMA(())] + ([pltpu.VMEM((EXPERT_TILE, d), F32)] if first else []),
    )
    return pl.pallas_call(
        functools.partial(_dispatch_kernel, first=first),
        grid_spec=grid_spec,
        out_shape=jax.ShapeDtypeStruct((n_rows, d), F32),
        input_output_aliases={} if first else {3: 0},
        compiler_params=_cparams("arbitrary"),
        name="moe_dispatch",
    )(slots, pad_rows, h2, *([] if first else [xs]))


def _expert_kernel(te_ref, nu_ref, x_ref, wg_ref, wu_ref, wd_ref, y_ref):
    t = pl.program_id(0)

    @pl.when(t < nu_ref[0])
    def _():
        x = x_ref[...].astype(BF16)
        hg = _dot(x, wg_ref[...].astype(BF16))
        hu = _dot(x, wu_ref[...].astype(BF16))
        act = _silu(hg) * hu
        y_ref[...] = _dot(act.astype(BF16), wd_ref[...].astype(BF16))

    @pl.when(t >= nu_ref[0])
    def _():
        y_ref[...] = jnp.zeros(y_ref.shape, F32)


def _moe_experts(xs, tile_expert, n_used, wg, wu, wd):
    n_rows, d = xs.shape
    f = wg.shape[2]
    n_tiles = n_rows // EXPERT_TILE
    used = lambda t, nu: jnp.minimum(t, nu[0] - 1)
    grid_spec = pltpu.PrefetchScalarGridSpec(
        num_scalar_prefetch=2,
        grid=(n_tiles,),
        in_specs=[pl.BlockSpec((EXPERT_TILE, d), lambda t, te, nu: (used(t, nu), 0)),
                  pl.BlockSpec((None, d, f), lambda t, te, nu: (te[t], 0, 0)),
                  pl.BlockSpec((None, d, f), lambda t, te, nu: (te[t], 0, 0)),
                  pl.BlockSpec((None, f, d), lambda t, te, nu: (te[t], 0, 0))],
        out_specs=pl.BlockSpec((EXPERT_TILE, d), lambda t, te, nu: (t, 0)),
    )
    return pl.pallas_call(
        _expert_kernel,
        grid_spec=grid_spec,
        out_shape=jax.ShapeDtypeStruct((n_rows, d), F32),
        compiler_params=_cparams("arbitrary"),
        name="moe_experts",
    )(tile_expert, n_used, xs, wg, wu, wd)


def _combine_kernel(slots_ref, ys_ref, meta_ref, x1_ref, g2_ref, o_ref, ybuf, sem):
    tb = x1_ref.shape[0]
    i = pl.program_id(0)

    def issue_block(blk, slot):
        def issue(t, c):
            for k in range(2):
                src = slots_ref[k * (slots_ref.shape[0] // 2) + blk * tb + t]
                _row_copy(ys_ref, src, ybuf.at[slot, k], t, sem.at[slot]).start()
            return c

        lax.fori_loop(0, tb, issue, 0, unroll=8)

    @pl.when(i == 0)
    def _():
        issue_block(0, 0)

    @pl.when(i + 1 < pl.num_programs(0))
    def _():
        issue_block(i + 1, lax.rem(i + 1, 2))

    slot = lax.rem(i, 2)

    def drain(t, c):
        for k in range(2):
            _row_copy(ys_ref, 0, ybuf.at[slot, k], 0, sem.at[slot]).wait()
        return c

    lax.fori_loop(0, tb, drain, 0, unroll=8)
    meta = meta_ref[...]
    lane = lax.broadcasted_iota(jnp.int32, meta.shape, 1)
    w1 = jnp.sum(jnp.where(lane == META_W1, meta, 0.0), axis=-1, keepdims=True)
    w2 = jnp.sum(jnp.where(lane == META_W2, meta, 0.0), axis=-1, keepdims=True)
    y = ybuf[slot]
    o_ref[...] = x1_ref[...] + g2_ref[...] * (w1 * y[0] + w2 * y[1])


def _moe_combine(ys, slots, meta, x1, g2, tb):
    m, d = x1.shape
    grid_spec = pltpu.PrefetchScalarGridSpec(
        num_scalar_prefetch=1,
        grid=(m // tb,),
        in_specs=[pl.BlockSpec(memory_space=pl.ANY),
                  pl.BlockSpec((tb, ROUTER_LANES), lambda i, s: (i, 0)),
                  pl.BlockSpec((tb, d), lambda i, s: (i, 0)),
                  _mod_spec(g2, tb)],
        out_specs=pl.BlockSpec((tb, d), lambda i, s: (i, 0)),
        scratch_shapes=[pltpu.VMEM((2, 2, tb, d), F32), pltpu.SemaphoreType.DMA((2,))],
    )
    return pl.pallas_call(
        _combine_kernel,
        grid_spec=grid_spec,
        out_shape=jax.ShapeDtypeStruct((m, d), F32),
        compiler_params=_cparams("arbitrary"),
        name="moe_combine",
    )(slots, ys, meta, x1, g2)


def _moe_sorted(groups, cnt, wg, wu, wd):
    m = sum(g["h2"].shape[0] for g in groups)
    n_tiles = -(-(2 * m) // EXPERT_TILE) + N_EXPERTS
    counts = cnt[0, N_GROUPS:N_GROUPS + N_EXPERTS].astype(jnp.int32)
    padded = (counts + EXPERT_TILE - 1) // EXPERT_TILE * EXPERT_TILE
    seg_end = jnp.cumsum(padded)
    seg_start = seg_end - padded
    n_used = (seg_end[-1] // EXPERT_TILE).astype(jnp.int32).reshape(1)
    tile = jnp.minimum(jnp.arange(n_tiles, dtype=jnp.int32), n_used[0] - 1)
    tile_expert = jnp.sum(seg_end[None, :] // EXPERT_TILE <= tile[:, None], axis=1).astype(jnp.int32)
    all_tiles = jnp.arange(n_tiles, dtype=jnp.int32)
    pad_rows = jnp.concatenate([jnp.where(padded > 0, seg_end - EXPERT_TILE, -1),
                                jnp.where(all_tiles >= n_used[0], all_tiles * EXPERT_TILE, -1)]).astype(jnp.int32)
    xs = None
    for g in groups:
        col = lambda c: g["meta_t"][c].astype(jnp.int32)
        g["slots"] = jnp.concatenate([seg_start[col(META_E1)] + col(META_R1), seg_start[col(META_E2)] + col(META_R2)])
        xs = _moe_dispatch(g["h2"], g["slots"], pad_rows, xs, n_tiles * EXPERT_TILE, g["tb"])
    ys = _moe_experts(xs, tile_expert, n_used, wg, wu, wd)
    return [_moe_combine(ys, g["slots"], g["meta"], g["x1"], g["g2"], g["tb"]) for g in groups]


def _pick(m, prefs):
    for t in prefs:
        if m % t == 0:
            return t
    return m


def _prep_layer_params(l, w_in, mu_shift, w0, w_decay_up, a0, w_iclr_up, w_gate_up, k_k, k_a, r_k, ln_x_w, ln_x_b,
                       q_norm, k_norm, lambda_q1, lambda_k1, lambda_q2, lambda_k2, subln, w_branch_a, w_branch_b,
                       w_out, w_router_coarse, b_router_coarse, w_router_fine, b_router_fine,
                       w_exp_gate, w_exp_up, w_exp_down, norm1, norm2):
    d = w_in.shape[1]
    wi = w_in[l]
    c_qkv = A_COLS + 2 * B_QK_WIDTH + B_V_WIDTH
    w_lora = jnp.zeros((LORA_IN, 3 * A_WIDTH), F32)
    w_lora = w_lora.at[0:DECAY_LORA, 0:A_WIDTH].set(w_decay_up[l])
    w_lora = w_lora.at[DECAY_LORA:DECAY_LORA + ICLR_LORA, A_WIDTH:2 * A_WIDTH].set(w_iclr_up[l])
    w_lora = w_lora.at[DECAY_LORA + ICLR_LORA:, 2 * A_WIDTH:].set(w_gate_up[l])
    lane = jnp.arange(LANES)
    sel64 = (lane[:, None] // A_HEAD_DIM == lane[None, :] // A_HEAD_DIM).astype(BF16)
    w_r = jnp.zeros((d, ROUTER_LANES), F32)
    w_r = w_r.at[:, 0:N_GROUPS].set(w_router_coarse[l])
    w_r = w_r.at[:, N_GROUPS:N_GROUPS + N_EXPERTS].set(
        jnp.transpose(w_router_fine[l], (1, 0, 2)).reshape(d, N_EXPERTS))
    w_rh = w_r.astype(BF16)
    w_rm = (w_r - w_rh.astype(F32)).astype(BF16)
    b_r = jnp.zeros((1, ROUTER_LANES), F32)
    b_r = b_r.at[0, 0:N_GROUPS].set(b_router_coarse[l])
    b_r = b_r.at[0, N_GROUPS:N_GROUPS + N_EXPERTS].set(b_router_fine[l].reshape(-1))
    row = lambda a: a.reshape(1, -1)
    return dict(
        w_in=wi.astype(BF16),
        mu_shift=row(mu_shift[l]), w_lora=w_lora.astype(BF16), w0=row(w0[l]), a0=row(a0[l]), k_k=row(k_k[l]),
        k_a=row(k_a[l]), r_k=row(r_k[l]), ln_x_w=row(ln_x_w[l]), ln_x_b=row(ln_x_b[l]), sel64=sel64,
        q_gain=row(jnp.tile(q_norm[l].reshape(-1), B_HEADS)), k_gain=row(jnp.tile(k_norm[l].reshape(-1), B_HEADS)),
        lam=(row(lambda_q1[l]), row(lambda_k1[l]), row(lambda_q2[l]), row(lambda_k2[l])), subln=row(subln[l]),
        w_a=w_branch_a[l].astype(BF16), w_b=w_branch_b[l].astype(BF16), w_out=w_out[l].astype(BF16),
        w_rh=w_rh, w_rm=w_rm, b_r=b_r,
        w_eg=w_exp_gate[l], w_eu=w_exp_up[l], w_ed=w_exp_down[l],
        norm1=row(norm1[l]), norm2=row(norm2[l]))


def _trunk_rows(x2, mods, p, tm):
    sh1, sc1 = mods[0], mods[1]
    tm_a = _pick(x2.shape[0], (tm,))
    c_qkv = 2 * B_QK_WIDTH + B_V_WIDTH
    h = _modnorm_rows(x2, p["norm1"], sc1, sh1, min(tm_a, 512))
    ua = _proj(h, p["w_in"], 0, A_COLS, tm_a, A_COLS // 2)
    uqkv = _proj(h, p["w_in"], A_COLS, c_qkv, tm_a, 1024)
    ug = _proj(h, p["w_in"], A_COLS + c_qkv, p["w_in"].shape[1] - A_COLS - c_qkv, tm_a, 1024)
    return ua, uqkv, ug


def _tail_rows(x2, ya, yb, ug, mods, p, tm, cnt0):
    _, _, g1, sh2, sc2, g2 = mods
    m = x2.shape[0]
    mg = _merge(ya, yb, ug, p["w_a"], p["w_b"], _pick(m, (tm,)))
    tb = _pick(m, (256, 128))
    x1, h2, meta, cnt, meta_t = _out_router(mg, x2, g1, p["norm2"], sc2, sh2, p["w_out"], p["w_rh"], p["w_rm"],
                                            p["b_r"], cnt0, tb)
    return dict(h2=h2, meta=meta, meta_t=meta_t, x1=x1, g2=g2, tb=tb), cnt


def kernel(x_prompt, x_sample, c_prompt, c_sample, cache_k, cache_v, state_wkv, state_shift, page_table, rel_bias, w_ada, b_ada, norm1, norm2, w_in, mu_shift, w0, w_decay_up, a0, w_iclr_up, w_gate_up, k_k, k_a, r_k, ln_x_w, ln_x_b, q_norm, k_norm, lambda_q1, lambda_k1, lambda_q2, lambda_k2, subln, w_branch_a, w_branch_b, w_out, w_router_coarse, b_router_coarse, w_router_fine, b_router_fine, w_exp_gate, w_exp_up, w_exp_down):
    depth = w_in.shape[0]
    bp, tp, d = x_prompt.shape
    bs, ts, _ = x_sample.shape
    n_new_pad = 8
    kv_new_rows = LANES // B_HEADS
    yp = x_prompt.reshape(bp * tp, d)
    ys = x_sample.reshape(bs * ts, d)
    c_all = jnp.concatenate([c_prompt, c_sample], axis=0)
    c_rows = -(-c_all.shape[0] // 8) * 8
    c_all = jnp.pad(c_all, ((0, c_rows - c_all.shape[0]), (0, 0)))
    outs = [[] for _ in range(8)]
    for l in range(depth):
        lam_init = 0.8 - 0.6 * math.exp(-0.3 * l)
        p = _prep_layer_params(l, w_in, mu_shift, w0, w_decay_up, a0, w_iclr_up, w_gate_up, k_k, k_a, r_k, ln_x_w,
                               ln_x_b, q_norm, k_norm, lambda_q1, lambda_k1, lambda_q2, lambda_k2, subln,
                               w_branch_a, w_branch_b, w_out, w_router_coarse, b_router_coarse, w_router_fine,
                               b_router_fine, w_exp_gate, w_exp_up, w_exp_down, norm1, norm2)
        ada = _ada(c_all, w_ada[l], b_ada[l])
        mods_p = [jnp.repeat(a, tp, axis=0) if bp > 1 else a for a in jnp.split(ada[0:bp], 6, axis=-1)]
        ua, uqkv, ug = _trunk_rows(yp, mods_p, p, _pick(bp * tp, (1024, 512, 256, 128)))
        ua3 = ua.reshape(bp, tp, A_COLS)
        ya, wkv_p = _rwkv(ua3, jnp.zeros((bp, 1, A_COLS), F32),
                          jnp.zeros((bp, A_HEADS, A_HEAD_DIM, A_HEAD_DIM), F32), p, 64,
                          2 if tp % 128 == 0 else 1, None)
        qt2, kf, kb, vf, vt = _qk_prep(uqkv, p["q_gain"], p["k_gain"], p["sel64"], 512, True)
        t_blk = _pick(tp, (512, 256, 128))
        yb = jnp.concatenate(
            [_flash_prompt(qt2[:, :, b * tp:(b + 1) * tp], kb[b * tp:(b + 1) * tp], vt[:, b * tp:(b + 1) * tp],
                           rel_bias, p["lam"], p["subln"], lam_init, t_blk) for b in range(bp)], axis=0)
        grp_p, cnt_p = _tail_rows(yp, ya.reshape(bp * tp, A_WIDTH), yb, ug, mods_p, p, 512,
                                  jnp.zeros((1, ROUTER_LANES), F32))
        outs[0].append(kf.reshape(bp, tp, B_HEADS, 2 * B_QK_DIM))
        outs[1].append(vf.reshape(bp, tp, B_HEADS, B_V_DIM))
        outs[2].append(wkv_p)
        outs[3].append(ua3[:, -1])
        mods_s = [jnp.repeat(a, ts, axis=0) for a in jnp.split(ada[bp:bp + bs], 6, axis=-1)]
        ua, uqkv, ug = _trunk_rows(ys, mods_s, p, 128)
        ua3 = ua.reshape(bs, ts, A_COLS)
        ua_pad = jnp.pad(ua3, ((0, 0), (0, n_new_pad - ts), (0, 0)))
        ya, wkv_s = _rwkv(ua_pad, state_shift[l][:, None, :], state_wkv[l], p, n_new_pad, 1, ts)
        ya = ya[:, 0:ts].reshape(bs * ts, A_WIDTH)
        q2, kf, kb, vf, vb = _qk_prep(uqkv, p["q_gain"], p["k_gain"], p["sel64"], _pick(bs * ts, (128,)), False)
        qd = q2.reshape(2, bs, ts, B_HEADS, 2 * B_QK_DIM).transpose(1, 0, 3, 2, 4).reshape(
            bs, 2 * B_HEADS * ts, 2 * B_QK_DIM)
        kn = jnp.pad(kb.reshape(bs, ts * B_HEADS, 2 * B_QK_DIM), ((0, 0), (0, (kv_new_rows - ts) * B_HEADS), (0, 0)))
        vn = jnp.pad(vb.reshape(bs, ts * B_HEADS, B_V_DIM), ((0, 0), (0, (kv_new_rows - ts) * B_HEADS), (0, 0)))
        rbrow = jnp.tile(jnp.repeat(rel_bias.T, ts, axis=0), (2, 1))
        od = _decode_attention(page_table, qd, kn, vn, rbrow, p["lam"], p["subln"], cache_k, cache_v, l, lam_init,
                               ts, _pick(page_table.shape[1], (16, 8, 4, 2, 1)))
        yb = od.reshape(bs, B_HEADS, ts, B_V_DIM).transpose(0, 2, 1, 3).reshape(bs * ts, B_V_WIDTH)
        grp_s, cnt_all = _tail_rows(ys, ya, yb, ug, mods_s, p, 128, cnt_p)
        yp, ys = _moe_sorted([grp_p, grp_s], cnt_all, p["w_eg"], p["w_eu"], p["w_ed"])
        outs[4].append(kf.reshape(bs, ts, B_HEADS, 2 * B_QK_DIM))
        outs[5].append(vf.reshape(bs, ts, B_HEADS, B_V_DIM))
        outs[6].append(wkv_s)
        outs[7].append(ua3[:, -1])
    st = [jnp.stack(o) for o in outs]
    return (yp.reshape(bp, tp, d), ys.reshape(bs, ts, d), st[0], st[1], st[2], st[3], st[4], st[5], st[6], st[7])
```
